```python
import math
import jax, jax.numpy as jnp
from jax import lax
import numpy as np

D_MODEL = 2048
BATCH = 4
SEQ = 8192
DEPTH = 1

MEM_LEN = 256
D_MIX = D_MODEL
POOL_WIDTH = D_MIX // 4
POOL_WINDOWS = (2, 4, 8, 16)
POOL_GROUPS = len(POOL_WINDOWS)
POOL_GROUP_DIM = POOL_WIDTH // POOL_GROUPS
MLA_V_DIM = 128
MLA_HEADS = (D_MIX // 2) // MLA_V_DIM
MLA_NOPE_DIM = 128
MLA_ROPE_DIM = 64
MLA_QK_DIM = MLA_NOPE_DIM + MLA_ROPE_DIM
Q_LORA_RANK = 512
KV_LORA_RANK = 256
X_HEADS = 4
X_WIDTH = D_MIX // 4
X_HEAD_DIM = X_WIDTH // X_HEADS
D_FF = 5632
CONV_WIDTH = 3
ROPE_THETA = 10000.0
NORM_EPS = 1e-6
Q_BLOCK = 128

IN_COLS = POOL_WIDTH + Q_LORA_RANK + KV_LORA_RANK + MLA_ROPE_DIM + X_WIDTH
IN_SPLITS = (
    POOL_WIDTH,
    POOL_WIDTH + Q_LORA_RANK,
    POOL_WIDTH + Q_LORA_RANK + KV_LORA_RANK,
    POOL_WIDTH + Q_LORA_RANK + KV_LORA_RANK + MLA_ROPE_DIM,
)

kernel_name = "hybrid_pool_mla_memxattn_convglu"


def rms_norm(x, g):
    xf = x.astype(jnp.float32)
    y = xf * lax.rsqrt(jnp.mean(xf * xf, axis=-1, keepdims=True) + NORM_EPS)
    return (y * g.astype(jnp.float32)).astype(x.dtype)


def apply_rope(x, pos):
    half = x.shape[-1] // 2
    inv_freq = 1.0 / (ROPE_THETA ** (jnp.arange(half, dtype=jnp.float32) / half))
    ang = pos.astype(jnp.float32)[:, None] * inv_freq[None, :]
    cos = jnp.cos(ang)[None, :, None, :]
    sin = jnp.sin(ang)[None, :, None, :]
    xf = x.astype(jnp.float32)
    x1, x2 = xf[..., :half], xf[..., half:]
    return jnp.concatenate([x1 * cos - x2 * sin, x2 * cos + x1 * sin], axis=-1).astype(x.dtype)


def pool_mixer(p, w_pool, pool_scale):
    B, S, _ = p.shape
    pf = p.astype(jnp.float32).reshape(B, S, POOL_GROUPS, POOL_GROUP_DIM)
    csum = jnp.cumsum(pf, axis=1)
    t = jnp.arange(S)
    outs = []
    for gi, w in enumerate(POOL_WINDOWS):
        cg = csum[:, :, gi]
        lag = jnp.pad(cg, ((0, 0), (w, 0), (0, 0)))[:, :S]
        cnt = jnp.minimum(t + 1, w).astype(jnp.float32)[None, :, None]
        outs.append((cg - lag) / cnt - pf[:, :, gi])
    d = jnp.stack(outs, axis=2).astype(p.dtype)
    y = jnp.einsum('bsgc,gcd->bsgd', d, w_pool).reshape(B, S, POOL_WIDTH)
    return y * pool_scale


def causal_attention_blocks(q, k, v):
    B, S, H, D = q.shape
    Dv = v.shape[-1]
    nb = S // Q_BLOCK
    scale = 1.0 / math.sqrt(D)
    qb = q.reshape(B, nb, Q_BLOCK, H, D).transpose(1, 0, 2, 3, 4)
    kpos = jnp.arange(S)

    def one_block(args):
        qblk, i = args
        qpos = i * Q_BLOCK + jnp.arange(Q_BLOCK)
        s = jnp.einsum('bqhd,bkhd->bhqk', qblk, k).astype(jnp.float32) * scale
        mask = kpos[None, :] <= qpos[:, None]
        s = jnp.where(mask[None, None], s, -jnp.inf)
        pr = jax.nn.softmax(s, axis=-1).astype(v.dtype)
        return jnp.einsum('bhqk,bkhd->bqhd', pr, v)

    o = lax.map(one_block, (qb, jnp.arange(nb)))
    return o.transpose(1, 0, 2, 3, 4).reshape(B, S, H, Dv)


def mla_mixer(q_lat, kv_lat, k_rope, pos, g_q_lat, w_q_up, g_kv_lat, w_kv_up, g_q_mla, g_k_mla):
    B, S, _ = q_lat.shape
    q = (rms_norm(q_lat, g_q_lat) @ w_q_up).reshape(B, S, MLA_HEADS, MLA_QK_DIM)
    kv = (rms_norm(kv_lat, g_kv_lat) @ w_kv_up).reshape(B, S, MLA_HEADS, MLA_NOPE_DIM + MLA_V_DIM)
    k_nope, v = kv[..., :MLA_NOPE_DIM], kv[..., MLA_NOPE_DIM:]
    k_r = jnp.broadcast_to(k_rope[:, :, None, :], (B, S, MLA_HEADS, MLA_ROPE_DIM))
    k = jnp.concatenate([k_nope, k_r], axis=-1)
    q = rms_norm(q, g_q_mla)
    k = rms_norm(k, g_k_mla)
    q = jnp.concatenate([q[..., :MLA_NOPE_DIM], apply_rope(q[..., MLA_NOPE_DIM:], pos)], axis=-1)
    k = jnp.concatenate([k[..., :MLA_NOPE_DIM], apply_rope(k[..., MLA_NOPE_DIM:], pos)], axis=-1)
    o = causal_attention_blocks(q, k, v)
    return o.reshape(B, S, MLA_HEADS * MLA_V_DIM)


def memory_cross_attention(xq, mem, g_mem, w_mem_kv, g_q_x, g_k_x):
    B, S, _ = xq.shape
    M = mem.shape[1]
    q = rms_norm(xq.reshape(B, S, X_HEADS, X_HEAD_DIM), g_q_x)
    mkv = rms_norm(mem, g_mem) @ w_mem_kv
    k = rms_norm(mkv[..., :X_WIDTH].reshape(B, M, X_HEADS, X_HEAD_DIM), g_k_x)
    v = mkv[..., X_WIDTH:].reshape(B, M, X_HEADS, X_HEAD_DIM)
    s = jnp.einsum('bshd,bmhd->bhsm', q, k).astype(jnp.float32) * (1.0 / math.sqrt(X_HEAD_DIM))
    pr = jax.nn.softmax(s, axis=-1).astype(v.dtype)
    o = jnp.einsum('bhsm,bmhd->bshd', pr, v)
    return o.reshape(B, S, X_WIDTH)


def conv_glu_ffn(h, w_gate, w_up, conv_w, conv_b, w_down):
    S = h.shape[1]
    g = h @ w_gate
    gp = jnp.pad(g, ((0, 0), (CONV_WIDTH - 1, 0), (0, 0)))
    gc = conv_b
    for j in range(CONV_WIDTH):
        gc = gc + conv_w[j] * gp[:, j:j + S]
    return (jax.nn.silu(gc) * (h @ w_up)) @ w_down


def setup_inputs(seed: int = 0) -> dict:
    key = jax.random.key(seed)
    ks = jax.random.split(key, 24)
    f32 = jnp.float32
    L = DEPTH

    def nrm(k, shape, scale):
        return jax.random.normal(k, shape, f32) * scale

    def gain(k, n):
        return 1.0 + 0.02 * jax.random.normal(k, (L, n), f32)

    return {
        "x": nrm(ks[0], (BATCH, SEQ, D_MODEL), 1.0),
        "mem": nrm(ks[1], (BATCH, MEM_LEN, D_MODEL), 1.0),
        "g_mix": gain(ks[2], D_MODEL),
        "w_in": nrm(ks[3], (L, D_MODEL, IN_COLS), D_MODEL ** -0.5),
        "g_q_lat": gain(ks[4], Q_LORA_RANK),
        "w_q_up": nrm(ks[5], (L, Q_LORA_RANK, MLA_HEADS * MLA_QK_DIM), Q_LORA_RANK ** -0.5),
        "g_kv_lat": gain(ks[6], KV_LORA_RANK),
        "w_kv_up": nrm(ks[7], (L, KV_LORA_RANK, MLA_HEADS * (MLA_NOPE_DIM + MLA_V_DIM)), KV_LORA_RANK ** -0.5),
        "g_q_mla": gain(ks[8], MLA_QK_DIM),
        "g_k_mla": gain(ks[9], MLA_QK_DIM),
        "w_pool": nrm(ks[10], (L, POOL_GROUPS, POOL_GROUP_DIM, POOL_GROUP_DIM), POOL_GROUP_DIM ** -0.5),
        "pool_scale": 1.0 + 0.1 * jax.random.normal(ks[11], (L, POOL_WIDTH), f32),
        "g_mem": gain(ks[12], D_MODEL),
        "w_mem_kv": nrm(ks[13], (L, D_MODEL, 2 * X_WIDTH), D_MODEL ** -0.5),
        "g_q_x": gain(ks[14], X_HEAD_DIM),
        "g_k_x": gain(ks[15], X_HEAD_DIM),
        "w_o": nrm(ks[16], (L, D_MIX, D_MODEL), D_MIX ** -0.5),
        "g_ffn": gain(ks[17], D_MODEL),
        "w_gate": nrm(ks[18], (L, D_MODEL, D_FF), D_MODEL ** -0.5),
        "w_up": nrm(ks[19], (L, D_MODEL, D_FF), D_MODEL ** -0.5),
        "conv_w": nrm(ks[20], (L, CONV_WIDTH, D_FF), CONV_WIDTH ** -0.5),
        "conv_b": nrm(ks[21], (L, D_FF), 0.01),
        "w_down": nrm(ks[22], (L, D_FF, D_MODEL), D_FF ** -0.5),
    }


def reference(x, mem, g_mix, w_in, g_q_lat, w_q_up, g_kv_lat, w_kv_up, g_q_mla, g_k_mla,
              w_pool, pool_scale, g_mem, w_mem_kv, g_q_x, g_k_x, w_o, g_ffn,
              w_gate, w_up, conv_w, conv_b, w_down):
    S = x.shape[1]
    pos = jnp.arange(S)
    for l in range(DEPTH):
        h = rms_norm(x, g_mix[l])
        z = h @ w_in[l]
        z_pool, z_q, z_kv, z_kr, z_mq = jnp.split(z, IN_SPLITS, axis=-1)
        y_pool = pool_mixer(z_pool, w_pool[l], pool_scale[l])
        y_mla = mla_mixer(z_q, z_kv, z_kr, pos, g_q_lat[l], w_q_up[l], g_kv_lat[l],
                          w_kv_up[l], g_q_mla[l], g_k_mla[l])
        y_mem = memory_cross_attention(z_mq, mem, g_mem[l], w_mem_kv[l], g_q_x[l], g_k_x[l])
        x = x + jnp.concatenate([y_pool, y_mla, y_mem], axis=-1) @ w_o[l]
        x = x + conv_glu_ffn(rms_norm(x, g_ffn[l]), w_gate[l], w_up[l], conv_w[l], conv_b[l], w_down[l])
    return x
```

```python
import functools
import math

import jax
import jax.numpy as jnp
from jax import lax
from jax.experimental import pallas as pl
from jax.experimental.pallas import tpu as pltpu

D_MODEL = 2048
MEM_LEN = 256
POOL_WIDTH = 512
POOL_WINDOWS = (2, 4, 8, 16)
POOL_GROUP_DIM = 128
POOL_HALO = 16
MLA_HEADS = 8
MLA_V_DIM = 128
MLA_NOPE_DIM = 128
MLA_ROPE_DIM = 64
MLA_QK_DIM = MLA_NOPE_DIM + MLA_ROPE_DIM
Q_LORA_RANK = 512
KV_LORA_RANK = 256
X_HEADS = 4
X_WIDTH = 512
X_HEAD_DIM = 128
D_FF = 5632
CONV_WIDTH = 3
ROPE_THETA = 10000.0
NORM_EPS = 1e-6

LANES = 128
SUBLANES = 8
HEAD_PAD = 2 * LANES
ROPE_HALF = MLA_ROPE_DIM // 2

_C_POOL = 0
_C_QLAT = _C_POOL + POOL_WIDTH
_C_KVLAT = _C_QLAT + Q_LORA_RANK
_C_KR = _C_KVLAT + KV_LORA_RANK
_C_MQ = _C_KR + LANES
IN_COLS_PAD = _C_MQ + X_WIDTH

TM_IN = 256
BQ = 256
BK = 256
TM_FF = 512
TF = 512
VMEM_LIMIT = 56 * 1024 * 1024

BF16 = jnp.bfloat16
F32 = jnp.float32


def _dot(a, b):
    return jnp.dot(a, b, preferred_element_type=F32)


def _dot_nt(a, b):
    return lax.dot_general(a, b, (((1,), (1,)), ((), ())), preferred_element_type=F32)


def _rms_rows(xf, g):
    return xf * lax.rsqrt(jnp.mean(xf * xf, axis=-1, keepdims=True) + NORM_EPS) * g


def _rope_group(grp, cos_t, sin_t):
    return grp * cos_t + pltpu.roll(grp, 2 * ROPE_HALF, 1) * sin_t


def _const_spec(shape):
    nd = len(shape)
    return pl.BlockSpec(shape, lambda *_: (0,) * nd)


def _mem_kv_kernel(mem_ref, g_mem_ref, wkT_ref, wv_ref, gk_ref, kT_ref, v_ref):
    h = _rms_rows(mem_ref[0], g_mem_ref[...]).astype(BF16)
    kT = _dot_nt(wkT_ref[...], h)
    for hd in range(X_HEADS):
        rows = slice(hd * X_HEAD_DIM, (hd + 1) * X_HEAD_DIM)
        kh = kT[rows, :]
        r = lax.rsqrt(jnp.mean(kh * kh, axis=0, keepdims=True) + NORM_EPS)
        kT_ref[0, rows, :] = (kh * r * gk_ref[...]).astype(BF16)
    v_ref[0] = _dot(h, wv_ref[...]).astype(BF16)


def _mem_kv(mem, g_mem, wkT, wv, gk_col):
    B = mem.shape[0]
    return pl.pallas_call(
        _mem_kv_kernel,
        grid=(B,),
        in_specs=[
            pl.BlockSpec((1, MEM_LEN, D_MODEL), lambda b: (b, 0, 0)),
            _const_spec((1, D_MODEL)),
            _const_spec((X_WIDTH, D_MODEL)),
            _const_spec((D_MODEL, X_WIDTH)),
            _const_spec((X_HEAD_DIM, 1)),
        ],
        out_specs=[
            pl.BlockSpec((1, X_WIDTH, MEM_LEN), lambda b: (b, 0, 0)),
            pl.BlockSpec((1, MEM_LEN, X_WIDTH), lambda b: (b, 0, 0)),
        ],
        out_shape=[
            jax.ShapeDtypeStruct((B, X_WIDTH, MEM_LEN), BF16),
            jax.ShapeDtypeStruct((B, MEM_LEN, X_WIDTH), BF16),
        ],
        compiler_params=pltpu.CompilerParams(
            dimension_semantics=("arbitrary",), vmem_limit_bytes=VMEM_LIMIT),
        name="mem_kv",
    )(mem, g_mem, wkT, wv, gk_col)


def _in_proj_kernel(x_ref, cos_ref, sin_ref, g_mix_ref, w_in_ref,
                    g_ql_ref, w_q_ref, gq_nope_ref, gq_grp_ref,
                    g_kvl_ref, w_k_ref, w_vT_ref, gk_nope_ref, gk_grp_ref,
                    w_pool_ref, pool_scale_ref,
                    g_qx_ref, kT_mem_ref, v_mem_ref,
                    q_ref, k_ref, vT_ref, yp_ref, ym_ref,
                    zp_ref):
    i = pl.program_id(1)
    tm = x_ref.shape[1]
    cos_t = cos_ref[...]
    sin_t = sin_ref[...]

    h = _rms_rows(x_ref[0], g_mix_ref[...]).astype(BF16)
    z = _dot(h, w_in_ref[...])
    z_pool = z[:, _C_POOL:_C_POOL + POOL_WIDTH]
    z_q = z[:, _C_QLAT:_C_QLAT + Q_LORA_RANK]
    z_kv = z[:, _C_KVLAT:_C_KVLAT + KV_LORA_RANK]
    kr = z[:, _C_KR:_C_KR + LANES]
    z_mq = z[:, _C_MQ:_C_MQ + X_WIDTH]

    @pl.when(i == 0)
    def _():
        zp_ref[0:POOL_HALO, :] = jnp.zeros((POOL_HALO, POOL_WIDTH), F32)

    @pl.when(i > 0)
    def _():
        zp_ref[0:POOL_HALO, :] = zp_ref[tm:tm + POOL_HALO, :]

    zp_ref[POOL_HALO:POOL_HALO + tm, :] = z_pool
    t = i * tm + lax.broadcasted_iota(jnp.int32, (tm, 1), 0)
    for g, w in enumerate(POOL_WINDOWS):
        cols = slice(g * POOL_GROUP_DIM, (g + 1) * POOL_GROUP_DIM)
        s = z_pool[:, cols]
        for j in range(1, w):
            s = s + zp_ref[POOL_HALO - j:POOL_HALO - j + tm, cols]
        cnt = jnp.minimum(t + 1, w).astype(F32)
        d = s / cnt - z_pool[:, cols]
        y = _dot(d.astype(BF16), w_pool_ref[g]) * pool_scale_ref[:, cols]
        yp_ref[0, :, cols] = y.astype(BF16)

    q_scale = 1.0 / math.sqrt(MLA_QK_DIM)
    hq = _rms_rows(z_q, g_ql_ref[...]).astype(BF16)
    qf = _dot(hq, w_q_ref[...])
    for hd in range(MLA_HEADS):
        c0 = hd * HEAD_PAD
        nope = qf[:, c0:c0 + LANES]
        grp = qf[:, c0 + LANES:c0 + HEAD_PAD]
        ssq = jnp.sum(nope * nope + 0.5 * (grp * grp), axis=-1, keepdims=True)
        r = lax.rsqrt(ssq * (1.0 / MLA_QK_DIM) + NORM_EPS) * q_scale
        q_ref[0, :, c0:c0 + LANES] = (nope * r * gq_nope_ref[...]).astype(BF16)
        rot = _rope_group(grp * gq_grp_ref[...], cos_t, sin_t)
        q_ref[0, :, c0 + LANES:c0 + HEAD_PAD] = (rot * r).astype(BF16)

    hkv = _rms_rows(z_kv, g_kvl_ref[...]).astype(BF16)
    kn = _dot(hkv, w_k_ref[...])
    vT_ref[0] = _dot_nt(w_vT_ref[...], hkv).astype(BF16)
    kr_ssq = 0.5 * jnp.sum(kr * kr, axis=-1, keepdims=True)
    kr_rot = _rope_group(kr * gk_grp_ref[...], cos_t, sin_t)
    for hd in range(MLA_HEADS):
        nope = kn[:, hd * LANES:(hd + 1) * LANES]
        ssq = jnp.sum(nope * nope, axis=-1, keepdims=True) + kr_ssq
        r = lax.rsqrt(ssq * (1.0 / MLA_QK_DIM) + NORM_EPS)
        c0 = hd * HEAD_PAD
        k_ref[0, :, c0:c0 + LANES] = (nope * r * gk_nope_ref[...]).astype(BF16)
        k_ref[0, :, c0 + LANES:c0 + HEAD_PAD] = (kr_rot * r).astype(BF16)

    x_scale = 1.0 / math.sqrt(X_HEAD_DIM)
    for hd in range(X_HEADS):
        cols = slice(hd * X_HEAD_DIM, (hd + 1) * X_HEAD_DIM)
        qn = (_rms_rows(z_mq[:, cols], g_qx_ref[...]) * x_scale).astype(BF16)
        s = _dot(qn, kT_mem_ref[0, cols, :])
        m = jnp.max(s, axis=-1, keepdims=True)
        p = jnp.exp(s - m)
        l = jnp.sum(p, axis=-1, keepdims=True)
        o = _dot(p.astype(BF16), v_mem_ref[0, :, cols]) / l
        ym_ref[0, :, cols] = o.astype(BF16)


def _in_proj(x, cos_t, sin_t, g_mix, w_in_p, g_ql, w_q_p, gq_nope, gq_grp,
             g_kvl, w_k, w_vT, gk_nope, gk_grp, w_pool, pool_scale, g_qx, kT_mem, v_mem):
    B, S, _ = x.shape
    tm = TM_IN
    n_kv = MLA_HEADS * MLA_V_DIM
    in_specs = [
        pl.BlockSpec((1, tm, D_MODEL), lambda b, i: (b, i, 0)),
        pl.BlockSpec((tm, LANES), lambda b, i: (i, 0)),
        pl.BlockSpec((tm, LANES), lambda b, i: (i, 0)),
        _const_spec((1, D_MODEL)),
        _const_spec((D_MODEL, IN_COLS_PAD)),
        _const_spec((1, Q_LORA_RANK)),
        _const_spec((Q_LORA_RANK, MLA_HEADS * HEAD_PAD)),
        _const_spec((1, LANES)),
        _const_spec((1, LANES)),
        _const_spec((1, KV_LORA_RANK)),
        _const_spec((KV_LORA_RANK, n_kv)),
        _const_spec((n_kv, KV_LORA_RANK)),
        _const_spec((1, LANES)),
        _const_spec((1, LANES)),
        _const_spec((len(POOL_WINDOWS), POOL_GROUP_DIM, POOL_GROUP_DIM)),
        _const_spec((1, POOL_WIDTH)),
        _const_spec((1, X_HEAD_DIM)),
        pl.BlockSpec((1, X_WIDTH, MEM_LEN), lambda b, i: (b, 0, 0)),
        pl.BlockSpec((1, MEM_LEN, X_WIDTH), lambda b, i: (b, 0, 0)),
    ]
    out_specs = [
        pl.BlockSpec((1, tm, MLA_HEADS * HEAD_PAD), lambda b, i: (b, i, 0)),
        pl.BlockSpec((1, tm, MLA_HEADS * HEAD_PAD), lambda b, i: (b, i, 0)),
        pl.BlockSpec((1, n_kv, tm), lambda b, i: (b, 0, i)),
        pl.BlockSpec((1, tm, POOL_WIDTH), lambda b, i: (b, i, 0)),
        pl.BlockSpec((1, tm, X_WIDTH), lambda b, i: (b, i, 0)),
    ]
    out_shape = [
        jax.ShapeDtypeStruct((B, S, MLA_HEADS * HEAD_PAD), BF16),
        jax.ShapeDtypeStruct((B, S, MLA_HEADS * HEAD_PAD), BF16),
        jax.ShapeDtypeStruct((B, n_kv, S), BF16),
        jax.ShapeDtypeStruct((B, S, POOL_WIDTH), BF16),
        jax.ShapeDtypeStruct((B, S, X_WIDTH), BF16),
    ]
    return pl.pallas_call(
        _in_proj_kernel,
        grid=(B, S // tm),
        in_specs=in_specs,
        out_specs=out_specs,
        out_shape=out_shape,
        scratch_shapes=[pltpu.VMEM((POOL_HALO + tm, POOL_WIDTH), F32)],
        compiler_params=pltpu.CompilerParams(
            dimension_semantics=("arbitrary", "arbitrary"), vmem_limit_bytes=VMEM_LIMIT),
        name="in_proj",
    )(x, cos_t, sin_t, g_mix, w_in_p, g_ql, w_q_p, gq_nope, gq_grp,
      g_kvl, w_k, w_vT, gk_nope, gk_grp, w_pool, pool_scale, g_qx, kT_mem, v_mem)


def _mla_attn_kernel(q_ref, k_ref, vT_ref, o_ref):
    i = pl.program_id(2)
    bq = q_ref.shape[1]
    q = q_ref[0]

    def step(carry, start, masked):
        m, l, acc = carry
        sT = _dot_nt(k_ref[0, pl.ds(start, BK), :], q)
        if masked:
            kpos = start + lax.broadcasted_iota(jnp.int32, (BK, bq), 0)
            qpos = i * bq + lax.broadcasted_iota(jnp.int32, (BK, bq), 1)
            sT = jnp.where(kpos <= qpos, sT, -jnp.inf)
        m_new = jnp.maximum(m, jnp.max(sT, axis=0, keepdims=True))
        p = jnp.exp(sT - m_new)
        alpha = jnp.exp(m - m_new)
        l = alpha * l + jnp.sum(p, axis=0, keepdims=True)
        acc = acc * alpha + _dot(vT_ref[0, :, pl.ds(start, BK)], p.astype(BF16))
        return m_new, l, acc

    init = (jnp.full((1, bq), -jnp.inf, F32), jnp.zeros((1, bq), F32),
            jnp.zeros((MLA_V_DIM, bq), F32))
    n_full = (i * bq) // BK
    carry = lax.fori_loop(
        0, n_full, lambda j, c: step(c, pl.multiple_of(j * BK, BK), False), init)
    m, l, acc = step(carry, pl.multiple_of(i * bq, BK), True)
    o_ref[0] = (acc / l).T.astype(BF16)


def _mla_attn(q, k, vT):
    B, S, _ = q.shape
    assert BQ == BK
    return pl.pallas_call(
        _mla_attn_kernel,
        grid=(B, MLA_HEADS, S // BQ),
        in_specs=[
            pl.BlockSpec((1, BQ, HEAD_PAD), lambda b, h, i: (b, i, h)),
            pl.BlockSpec((1, S, HEAD_PAD), lambda b, h, i: (b, 0, h)),
            pl.BlockSpec((1, MLA_V_DIM, S), lambda b, h, i: (b, h, 0)),
        ],
        out_specs=pl.BlockSpec((1, BQ, MLA_V_DIM), lambda b, h, i: (b, i, h)),
        out_shape=jax.ShapeDtypeStruct((B, S, MLA_HEADS * MLA_V_DIM), BF16),
        compiler_params=pltpu.CompilerParams(
            dimension_semantics=("arbitrary", "arbitrary", "arbitrary"),
            vmem_limit_bytes=VMEM_LIMIT),
        name="mla_attn",
    )(q, k, vT)


def _out_ffn_kernel(x_ref, yp_ref, ya_ref, ym_ref, w_o_ref, g_ffn_ref,
                    w_gate_ref, w_up_ref, conv_w_ref, conv_b_ref, w_down_ref,
                    o_ref, h2_ref, gbuf_ref, carry_ref):
    i = pl.program_id(1)
    c = pl.program_id(2)
    tm = x_ref.shape[1]
    n_mla = MLA_HEADS * MLA_V_DIM

    @pl.when(c == 0)
    def _():
        x1 = (x_ref[0]
              + _dot(yp_ref[0], w_o_ref[0:POOL_WIDTH, :])
              + _dot(ya_ref[0], w_o_ref[POOL_WIDTH:POOL_WIDTH + n_mla, :])
              + _dot(ym_ref[0], w_o_ref[POOL_WIDTH + n_mla:, :]))
        o_ref[0] = x1
        h2_ref[...] = _rms_rows(x1, g_ffn_ref[...]).astype(BF16)

    h2 = h2_ref[...]
    g = _dot(h2, w_gate_ref[...])
    u = _dot(h2, w_up_ref[...])

    @pl.when(i == 0)
    def _():
        gbuf_ref[0:SUBLANES, :] = jnp.zeros((SUBLANES, TF), F32)

    @pl.when(i > 0)
    def _():
        gbuf_ref[0:SUBLANES, :] = carry_ref[c]

    gbuf_ref[SUBLANES:SUBLANES + tm, :] = g
    carry_ref[c] = g[tm - SUBLANES:tm, :]
    gc = conv_b_ref[...] + conv_w_ref[CONV_WIDTH - 1:CONV_WIDTH, :] * g
    for j in range(CONV_WIDTH - 1):
        lag = CONV_WIDTH - 1 - j
        gc = gc + conv_w_ref[j:j + 1, :] * gbuf_ref[SUBLANES - lag:SUBLANES - lag + tm, :]
    act = (gc / (1.0 + jnp.exp(-gc))) * u
    o_ref[0] += _dot(act.astype(BF16), w_down_ref[...])


def _out_ffn(x, yp, ya, ym, w_o, g_ffn, w_gate, w_up, conv_w, conv_b, w_down):
    B, S, _ = x.shape
    tm = TM_FF
    n_mla = MLA_HEADS * MLA_V_DIM
    n_ff = D_FF // TF
    assert n_ff * TF == D_FF
    return pl.pallas_call(
        _out_ffn_kernel,
        grid=(B, S // tm, n_ff),
        in_specs=[
            pl.BlockSpec((1, tm, D_MODEL), lambda b, i, c: (b, i, 0)),
            pl.BlockSpec((1, tm, POOL_WIDTH), lambda b, i, c: (b, i, 0)),
            pl.BlockSpec((1, tm, n_mla), lambda b, i, c: (b, i, 0)),
            pl.BlockSpec((1, tm, X_WIDTH), lambda b, i, c: (b, i, 0)),
            _const_spec((D_MODEL, D_MODEL)),
            _const_spec((1, D_MODEL)),
            pl.BlockSpec((D_MODEL, TF), lambda b, i, c: (0, c)),
            pl.BlockSpec((D_MODEL, TF), lambda b, i, c: (0, c)),
            pl.BlockSpec((CONV_WIDTH, TF), lambda b, i, c: (0, c)),
            pl.BlockSpec((1, TF), lambda b, i, c: (0, c)),
            pl.BlockSpec((TF, D_MODEL), lambda b, i, c: (c, 0)),
        ],
        out_specs=pl.BlockSpec((1, tm, D_MODEL), lambda b, i, c: (b, i, 0)),
        out_shape=jax.ShapeDtypeStruct((B, S, D_MODEL), F32),
        scratch_shapes=[
            pltpu.VMEM((tm, D_MODEL), BF16),
            pltpu.VMEM((SUBLANES + tm, TF), F32),
            pltpu.VMEM((n_ff, SUBLANES, TF), F32),
        ],
        compiler_params=pltpu.CompilerParams(
            dimension_semantics=("arbitrary", "arbitrary", "arbitrary"),
            vmem_limit_bytes=VMEM_LIMIT),
        name="out_ffn",
    )(x, yp, ya, ym, w_o, g_ffn, w_gate, w_up, conv_w, conv_b, w_down)


def _dup_rope_cols(w):
    x1, x2 = w[..., :ROPE_HALF], w[..., ROPE_HALF:]
    return jnp.concatenate([x1, x2, x2, x1], axis=-1)


def _rope_tables(S):
    inv_freq = 1.0 / (ROPE_THETA ** (jnp.arange(ROPE_HALF, dtype=F32) / ROPE_HALF))
    ang = jnp.arange(S).astype(F32)[:, None] * inv_freq[None, :]
    cos, sin = jnp.cos(ang), jnp.sin(ang)
    zeros = jnp.zeros((S, 2 * ROPE_HALF), F32)
    return (jnp.concatenate([cos, cos, zeros], axis=-1),
            jnp.concatenate([-sin, sin, zeros], axis=-1))


def _layer(x, mem, g_mix, w_in, g_q_lat, w_q_up, g_kv_lat, w_kv_up, g_q_mla, g_k_mla,
           w_pool, pool_scale, g_mem, w_mem_kv, g_q_x, g_k_x, w_o, g_ffn,
           w_gate, w_up, conv_w, conv_b, w_down):
    S = x.shape[1]
    row = lambda v: v.reshape(1, -1)

    s0, s1, s2, s3 = _C_QLAT, _C_KVLAT, _C_KR, _C_KR + MLA_ROPE_DIM
    w_in_p = jnp.concatenate(
        [w_in[:, :s2], _dup_rope_cols(w_in[:, s2:s3]), w_in[:, s3:]], axis=1).astype(BF16)

    wq = w_q_up.reshape(Q_LORA_RANK, MLA_HEADS, MLA_QK_DIM)
    w_q_p = jnp.concatenate(
        [wq[..., :MLA_NOPE_DIM], _dup_rope_cols(wq[..., MLA_NOPE_DIM:])], axis=-1
    ).reshape(Q_LORA_RANK, MLA_HEADS * HEAD_PAD).astype(BF16)

    wkv = w_kv_up.reshape(KV_LORA_RANK, MLA_HEADS, MLA_NOPE_DIM + MLA_V_DIM)
    w_k = wkv[..., :MLA_NOPE_DIM].reshape(KV_LORA_RANK, -1).astype(BF16)
    w_vT = wkv[..., MLA_NOPE_DIM:].reshape(KV_LORA_RANK, -1).T.astype(BF16)

    cos_t, sin_t = _rope_tables(S)

    kT_mem, v_mem = _mem_kv(
        mem, row(g_mem), w_mem_kv[:, :X_WIDTH].T.astype(BF16), w_mem_kv[:, X_WIDTH:].astype(BF16),
        g_k_x.reshape(-1, 1))

    q, k, vT, yp, ym = _in_proj(
        x, cos_t, sin_t, row(g_mix), w_in_p,
        row(g_q_lat), w_q_p, row(g_q_mla[:MLA_NOPE_DIM]), row(_dup_rope_cols(g_q_mla[MLA_NOPE_DIM:])),
        row(g_kv_lat), w_k, w_vT, row(g_k_mla[:MLA_NOPE_DIM]), row(_dup_rope_cols(g_k_mla[MLA_NOPE_DIM:])),
        w_pool.astype(BF16), row(pool_scale), row(g_q_x), kT_mem, v_mem)

    ya = _mla_attn(q, k, vT)

    return _out_ffn(x, yp, ya, ym, w_o.astype(BF16), row(g_ffn),
                    w_gate.astype(BF16), w_up.astype(BF16), conv_w, row(conv_b), w_down.astype(BF16))


def kernel(x, mem, g_mix, w_in, g_q_lat, w_q_up, g_kv_lat, w_kv_up, g_q_mla, g_k_mla, w_pool, pool_scale, g_mem, w_mem_kv, g_q_x, g_k_x, w_o, g_ffn, w_gate, w_up, conv_w, conv_b, w_down):
    depth = g_mix.shape[0]
    for l in range(depth):
        x = _layer(x, mem, g_mix[l], w_in[l], g_q_lat[l], w_q_up[l], g_kv_lat[l], w_kv_up[l],
                   g_q_mla[l], g_k_mla[l], w_pool[l], pool_scale[l], g_mem[l], w_mem_kv[l],
                   g_q_x[l], g_k_x[l], w_o[l], g_ffn[l], w_gate[l], w_up[l], conv_w[l],
                   conv_b[l], w_down[l])
    return x
```

```python
import functools
import math

import jax
import jax.numpy as jnp
from jax import lax
from jax.experimental import pallas as pl
from jax.experimental.pallas import tpu as pltpu

D_MODEL = 2048
MEM_LEN = 256
POOL_WIDTH = 512
POOL_WINDOWS = (2, 4, 8, 16)
POOL_GROUP_DIM = 128
POOL_HALO = 16
MLA_HEADS = 8
MLA_V_DIM = 128
MLA_NOPE_DIM = 128
MLA_ROPE_DIM = 64
MLA_QK_DIM = MLA_NOPE_DIM + MLA_ROPE_DIM
Q_LORA_RANK = 512
KV_LORA_RANK = 256
X_HEADS = 4
X_WIDTH = 512
X_HEAD_DIM = 128
D_FF = 5632
CONV_WIDTH = 3
ROPE_THETA = 10000.0
NORM_EPS = 1e-6

LANES = 128
SUBLANES = 8
HEAD_PAD = 2 * LANES
ROPE_HALF = MLA_ROPE_DIM // 2

_C_POOL = 0
_C_QLAT = _C_POOL + POOL_WIDTH
_C_KVLAT = _C_QLAT + Q_LORA_RANK
_C_KR = _C_KVLAT + KV_LORA_RANK
_C_MQ = _C_KR + LANES
IN_COLS_PAD = _C_MQ + X_WIDTH

TM_IN = 256
BQ = 1024
QC = 256
N_CHAIN = BQ // QC
BK = BQ // 2
ROW_CHUNK = 32
TM_FF = 512
TF = 512
VMEM_LIMIT = 56 * 1024 * 1024

BF16 = jnp.bfloat16
F32 = jnp.float32


def _dot(a, b):
    return jnp.dot(a, b, preferred_element_type=F32)


def _dot_nt(a, b):
    return lax.dot_general(a, b, (((1,), (1,)), ((), ())), preferred_element_type=F32)


def _rms_rows(xf, g):
    return xf * lax.rsqrt(jnp.mean(xf * xf, axis=-1, keepdims=True) + NORM_EPS) * g


def _rope_group(grp, cos_t, sin_t):
    return grp * cos_t + pltpu.roll(grp, 2 * ROPE_HALF, 1) * sin_t


def _const_spec(shape):
    nd = len(shape)
    return pl.BlockSpec(shape, lambda *_: (0,) * nd)


def _mem_kv_kernel(mem_ref, g_mem_ref, wkT_ref, wv_ref, gk_ref, kT_ref, v_ref):
    h = _rms_rows(mem_ref[0], g_mem_ref[...]).astype(BF16)
    kT = _dot_nt(wkT_ref[...], h)
    for hd in range(X_HEADS):
        rows = slice(hd * X_HEAD_DIM, (hd + 1) * X_HEAD_DIM)
        kh = kT[rows, :]
        r = lax.rsqrt(jnp.mean(kh * kh, axis=0, keepdims=True) + NORM_EPS)
        kT_ref[0, rows, :] = (kh * r * gk_ref[...]).astype(BF16)
    v_ref[0] = _dot(h, wv_ref[...]).astype(BF16)


def _mem_kv(mem, g_mem, wkT, wv, gk_col):
    B = mem.shape[0]
    return pl.pallas_call(
        _mem_kv_kernel,
        grid=(B,),
        in_specs=[
            pl.BlockSpec((1, MEM_LEN, D_MODEL), lambda b: (b, 0, 0)),
            _const_spec((1, D_MODEL)),
            _const_spec((X_WIDTH, D_MODEL)),
            _const_spec((D_MODEL, X_WIDTH)),
            _const_spec((X_HEAD_DIM, 1)),
        ],
        out_specs=[
            pl.BlockSpec((1, X_WIDTH, MEM_LEN), lambda b: (b, 0, 0)),
            pl.BlockSpec((1, MEM_LEN, X_WIDTH), lambda b: (b, 0, 0)),
        ],
        out_shape=[
            jax.ShapeDtypeStruct((B, X_WIDTH, MEM_LEN), BF16),
            jax.ShapeDtypeStruct((B, MEM_LEN, X_WIDTH), BF16),
        ],
        compiler_params=pltpu.CompilerParams(
            dimension_semantics=("arbitrary",), vmem_limit_bytes=VMEM_LIMIT),
        name="mem_kv",
    )(mem, g_mem, wkT, wv, gk_col)


def _in_proj_kernel(x_ref, cos_ref, sin_ref, g_mix_ref, w_in_ref,
                    g_ql_ref, w_q_ref, gq_nope_ref, gq_grp_ref,
                    g_kvl_ref, w_k_ref, w_vT_ref, gk_nope_ref, gk_grp_ref,
                    w_pool_ref, pool_scale_ref,
                    g_qx_ref, kT_mem_ref, v_mem_ref,
                    q_ref, k_ref, vT_ref, yp_ref, ym_ref,
                    zp_ref):
    i = pl.program_id(1)
    tm = x_ref.shape[1]
    cos_t = cos_ref[...]
    sin_t = sin_ref[...]

    @pl.when(i == 0)
    def _():
        zp_ref[0:POOL_HALO, :] = jnp.zeros((POOL_HALO, POOL_WIDTH), F32)

    @pl.when(i > 0)
    def _():
        zp_ref[0:POOL_HALO, :] = zp_ref[tm:tm + POOL_HALO, :]

    h = _rms_rows(x_ref[0], g_mix_ref[...]).astype(BF16)
    z = _dot(h, w_in_ref[...])
    z_pool = z[:, _C_POOL:_C_POOL + POOL_WIDTH]
    z_q = z[:, _C_QLAT:_C_QLAT + Q_LORA_RANK]
    z_kv = z[:, _C_KVLAT:_C_KVLAT + KV_LORA_RANK]
    kr = z[:, _C_KR:_C_KR + LANES]
    z_mq = z[:, _C_MQ:_C_MQ + X_WIDTH]

    zp_ref[POOL_HALO:POOL_HALO + tm, :] = z_pool
    t = i * tm + lax.broadcasted_iota(jnp.int32, (tm, 1), 0)
    for g, w in enumerate(POOL_WINDOWS):
        cols = slice(g * POOL_GROUP_DIM, (g + 1) * POOL_GROUP_DIM)
        s = z_pool[:, cols]
        for j in range(1, w):
            s = s + zp_ref[POOL_HALO - j:POOL_HALO - j + tm, cols]
        cnt = jnp.minimum(t + 1, w).astype(F32)
        d = s / cnt - z_pool[:, cols]
        y = _dot(d.astype(BF16), w_pool_ref[g]) * pool_scale_ref[:, cols]
        yp_ref[0, :, cols] = y.astype(BF16)

    q_scale = math.log2(math.e) / math.sqrt(MLA_QK_DIM)
    hq = _rms_rows(z_q, g_ql_ref[...]).astype(BF16)
    qf = _dot(hq, w_q_ref[...])
    for hd in range(MLA_HEADS):
        c0 = hd * HEAD_PAD
        nope = qf[:, c0:c0 + LANES]
        grp = qf[:, c0 + LANES:c0 + HEAD_PAD]
        ssq = jnp.sum(nope * nope + 0.5 * (grp * grp), axis=-1, keepdims=True)
        r = lax.rsqrt(ssq * (1.0 / MLA_QK_DIM) + NORM_EPS) * q_scale
        q_ref[0, :, c0:c0 + LANES] = (nope * r * gq_nope_ref[...]).astype(BF16)
        rot = _rope_group(grp * gq_grp_ref[...], cos_t, sin_t)
        q_ref[0, :, c0 + LANES:c0 + HEAD_PAD] = (rot * r).astype(BF16)

    hkv = _rms_rows(z_kv, g_kvl_ref[...]).astype(BF16)
    kn = _dot(hkv, w_k_ref[...])
    vT_ref[0] = _dot_nt(w_vT_ref[...], hkv).astype(BF16)
    kr_ssq = 0.5 * jnp.sum(kr * kr, axis=-1, keepdims=True)
    kr_rot = _rope_group(kr * gk_grp_ref[...], cos_t, sin_t)
    for hd in range(MLA_HEADS):
        nope = kn[:, hd * LANES:(hd + 1) * LANES]
        ssq = jnp.sum(nope * nope, axis=-1, keepdims=True) + kr_ssq
        r = lax.rsqrt(ssq * (1.0 / MLA_QK_DIM) + NORM_EPS)
        c0 = hd * HEAD_PAD
        k_ref[0, :, c0:c0 + LANES] = (nope * r * gk_nope_ref[...]).astype(BF16)
        k_ref[0, :, c0 + LANES:c0 + HEAD_PAD] = (kr_rot * r).astype(BF16)

    x_scale = 1.0 / math.sqrt(X_HEAD_DIM)
    for hd in range(X_HEADS):
        cols = slice(hd * X_HEAD_DIM, (hd + 1) * X_HEAD_DIM)
        qn = (_rms_rows(z_mq[:, cols], g_qx_ref[...]) * x_scale).astype(BF16)
        s = _dot(qn, kT_mem_ref[0, cols, :])
        m = jnp.max(s, axis=-1, keepdims=True)
        p = jnp.exp(s - m)
        l = jnp.sum(p, axis=-1, keepdims=True)
        o = _dot(p.astype(BF16), v_mem_ref[0, :, cols]) / l
        ym_ref[0, :, cols] = o.astype(BF16)


def _in_proj(x, cos_t, sin_t, g_mix, w_in_p, g_ql, w_q_p, gq_nope, gq_grp,
             g_kvl, w_k, w_vT, gk_nope, gk_grp, w_pool, pool_scale, g_qx, kT_mem, v_mem):
    B, S, _ = x.shape
    tm = TM_IN
    n_kv = MLA_HEADS * MLA_V_DIM
    in_specs = [
        pl.BlockSpec((1, tm, D_MODEL), lambda b, i: (b, i, 0)),
        pl.BlockSpec((tm, LANES), lambda b, i: (i, 0)),
        pl.BlockSpec((tm, LANES), lambda b, i: (i, 0)),
        _const_spec((1, D_MODEL)),
        _const_spec((D_MODEL, IN_COLS_PAD)),
        _const_spec((1, Q_LORA_RANK)),
        _const_spec((Q_LORA_RANK, MLA_HEADS * HEAD_PAD)),
        _const_spec((1, LANES)),
        _const_spec((1, LANES)),
        _const_spec((1, KV_LORA_RANK)),
        _const_spec((KV_LORA_RANK, n_kv)),
        _const_spec((n_kv, KV_LORA_RANK)),
        _const_spec((1, LANES)),
        _const_spec((1, LANES)),
        _const_spec((len(POOL_WINDOWS), POOL_GROUP_DIM, POOL_GROUP_DIM)),
        _const_spec((1, POOL_WIDTH)),
        _const_spec((1, X_HEAD_DIM)),
        pl.BlockSpec((1, X_WIDTH, MEM_LEN), lambda b, i: (b, 0, 0)),
        pl.BlockSpec((1, MEM_LEN, X_WIDTH), lambda b, i: (b, 0, 0)),
    ]
    out_specs = [
        pl.BlockSpec((1, tm, MLA_HEADS * HEAD_PAD), lambda b, i: (b, i, 0)),
        pl.BlockSpec((1, tm, MLA_HEADS * HEAD_PAD), lambda b, i: (b, i, 0)),
        pl.BlockSpec((1, n_kv, tm), lambda b, i: (b, 0, i)),
        pl.BlockSpec((1, tm, POOL_WIDTH), lambda b, i: (b, i, 0)),
        pl.BlockSpec((1, tm, X_WIDTH), lambda b, i: (b, i, 0)),
    ]
    out_shape = [
        jax.ShapeDtypeStruct((B, S, MLA_HEADS * HEAD_PAD), BF16),
        jax.ShapeDtypeStruct((B, S, MLA_HEADS * HEAD_PAD), BF16),
        jax.ShapeDtypeStruct((B, n_kv, S), BF16),
        jax.ShapeDtypeStruct((B, S, POOL_WIDTH), BF16),
        jax.ShapeDtypeStruct((B, S, X_WIDTH), BF16),
    ]
    return pl.pallas_call(
        _in_proj_kernel,
        grid=(B, S // tm),
        in_specs=in_specs,
        out_specs=out_specs,
        out_shape=out_shape,
        scratch_shapes=[pltpu.VMEM((POOL_HALO + tm, POOL_WIDTH), F32)],
        compiler_params=pltpu.CompilerParams(
            dimension_semantics=("arbitrary", "arbitrary"), vmem_limit_bytes=VMEM_LIMIT),
        name="in_proj",
    )(x, cos_t, sin_t, g_mix, w_in_p, g_ql, w_q_p, gq_nope, gq_grp,
      g_kvl, w_k, w_vT, gk_nope, gk_grp, w_pool, pool_scale, g_qx, kT_mem, v_mem)


def _mla_attn_kernel(q_ref, k_ref, vT_ref, o_ref, s_ref, mb_ref, p_ref, m_ref, l_ref, acc_ref):
    i = pl.program_id(2)
    chains = tuple(range(N_CHAIN))

    def scores(slot, start, cs):
        k = k_ref[0, pl.ds(start, BK), :]
        for c in cs:
            sT = _dot_nt(k, q_ref[0, c * QC:(c + 1) * QC, :])
            s_ref[slot, c] = sT
            mb_ref[slot, c] = jnp.max(sT, axis=0, keepdims=True)

    def mask(slot, c, off):
        rr = lax.broadcasted_iota(jnp.int32, (BK, QC), 0)
        qq = lax.broadcasted_iota(jnp.int32, (BK, QC), 1)
        sT = jnp.where(rr <= qq + off, s_ref[slot, c], -jnp.inf)
        s_ref[slot, c] = sT
        mb_ref[slot, c] = jnp.max(sT, axis=0, keepdims=True)

    def update(slot, start, cs):
        for c in cs:
            m_old = m_ref[c]
            m_new = jnp.maximum(m_old, mb_ref[slot, c])
            alpha = jnp.exp2(m_old - m_new)
            lacc = jnp.zeros((ROW_CHUNK, QC), F32)
            for r in range(0, BK, ROW_CHUNK):
                p = jnp.exp2(s_ref[slot, c, r:r + ROW_CHUNK, :] - m_new)
                lacc = lacc + p
                p_ref[c, r:r + ROW_CHUNK, :] = p.astype(BF16)
            l_ref[c] = alpha * l_ref[c] + jnp.sum(lacc, axis=0, keepdims=True)
            pv = _dot(vT_ref[0, :, pl.ds(start, BK)], p_ref[c])
            acc_ref[c] = acc_ref[c] * alpha + pv
            m_ref[c] = m_new

    m_ref[...] = jnp.full(m_ref.shape, -jnp.inf, F32)
    l_ref[...] = jnp.zeros(l_ref.shape, F32)
    acc_ref[...] = jnp.zeros(acc_ref.shape, F32)

    scores(0, 0, chains)

    def trip(t, carry):
        b0 = pl.multiple_of(t * (2 * BK), 2 * BK)
        scores(1, b0 + BK, chains)
        update(0, b0, chains)

        @pl.when(t >= 0)
        def _():
            scores(0, b0 + 2 * BK, chains)
            update(1, b0 + BK, chains)

        return carry

    lax.fori_loop(0, i, trip, 0)

    base = pl.multiple_of(i * BQ, BQ)
    n_diag = BQ // BK
    for d in range(n_diag):
        slot = d % 2
        live = tuple(c for c in chains if (c * QC) // BK >= d)
        if d + 1 < n_diag:
            scores(1 - slot, base + (d + 1) * BK, tuple(c for c in chains if (c * QC) // BK >= d + 1))
        for c in live:
            if (c * QC) // BK == d:
                mask(slot, c, c * QC - d * BK)
        update(slot, base + d * BK, live)

    for c in chains:
        o_ref[0, c * QC:(c + 1) * QC, :] = (acc_ref[c] / l_ref[c]).T.astype(BF16)


def _mla_attn(q, k, vT):
    B, S, _ = q.shape
    assert BQ == 2 * BK and BQ % QC == 0 and BK % QC == 0 and BK % ROW_CHUNK == 0
    return pl.pallas_call(
        _mla_attn_kernel,
        grid=(B, MLA_HEADS, S // BQ),
        in_specs=[
            pl.BlockSpec((1, BQ, HEAD_PAD), lambda b, h, i: (b, i, h)),
            pl.BlockSpec((1, S, HEAD_PAD), lambda b, h, i: (b, 0, h)),
            pl.BlockSpec((1, MLA_V_DIM, S), lambda b, h, i: (b, h, 0)),
        ],
        out_specs=pl.BlockSpec((1, BQ, MLA_V_DIM), lambda b, h, i: (b, i, h)),
        out_shape=jax.ShapeDtypeStruct((B, S, MLA_HEADS * MLA_V_DIM), BF16),
        scratch_shapes=[
            pltpu.VMEM((2, N_CHAIN, BK, QC), F32),
            pltpu.VMEM((2, N_CHAIN, 1, QC), F32),
            pltpu.VMEM((N_CHAIN, BK, QC), BF16),
            pltpu.VMEM((N_CHAIN, 1, QC), F32),
            pltpu.VMEM((N_CHAIN, 1, QC), F32),
            pltpu.VMEM((N_CHAIN, MLA_V_DIM, QC), F32),
        ],
        compiler_params=pltpu.CompilerParams(
            dimension_semantics=("arbitrary", "arbitrary", "arbitrary"),
            vmem_limit_bytes=VMEM_LIMIT),
        name="mla_attn",
    )(q, k, vT)


def _out_ffn_kernel(x_ref, yp_ref, ya_ref, ym_ref, w_o_ref, g_ffn_ref,
                    w_gate_ref, w_up_ref, conv_w_ref, conv_b_ref, w_down_ref,
                    o_ref, h2_ref, gbuf_ref, carry_ref):
    i = pl.program_id(1)
    c = pl.program_id(2)
    tm = x_ref.shape[1]
    n_mla = MLA_HEADS * MLA_V_DIM

    @pl.when(c == 0)
    def _():
        x1 = (x_ref[0]
              + _dot(yp_ref[0], w_o_ref[0:POOL_WIDTH, :])
              + _dot(ya_ref[0], w_o_ref[POOL_WIDTH:POOL_WIDTH + n_mla, :])
              + _dot(ym_ref[0], w_o_ref[POOL_WIDTH + n_mla:, :]))
        o_ref[0] = x1
        h2_ref[...] = _rms_rows(x1, g_ffn_ref[...]).astype(BF16)

        @pl.when(i == 0)
        def _():
            carry_ref[...] = jnp.zeros(carry_ref.shape, F32)

    h2 = h2_ref[...]
    g = _dot(h2, w_gate_ref[...])
    u = _dot(h2, w_up_ref[...])

    gbuf_ref[0:SUBLANES, :] = carry_ref[c]
    gbuf_ref[SUBLANES:SUBLANES + tm, :] = g
    carry_ref[c] = g[tm - SUBLANES:tm, :]
    gc = conv_b_ref[...] + conv_w_ref[CONV_WIDTH - 1:CONV_WIDTH, :] * g
    for j in range(CONV_WIDTH - 1):
        lag = CONV_WIDTH - 1 - j
        gc = gc + conv_w_ref[j:j + 1, :] * gbuf_ref[SUBLANES - lag:SUBLANES - lag + tm, :]
    act = (gc / (1.0 + jnp.exp(-gc))) * u
    o_ref[0] += _dot(act.astype(BF16), w_down_ref[...])


def _out_ffn(x, yp, ya, ym, w_o, g_ffn, w_gate, w_up, conv_w, conv_b, w_down):
    B, S, _ = x.shape
    tm = TM_FF
    n_mla = MLA_HEADS * MLA_V_DIM
    n_ff = D_FF // TF
    assert n_ff * TF == D_FF
    return pl.pallas_call(
        _out_ffn_kernel,
        grid=(B, S // tm, n_ff),
        in_specs=[
            pl.BlockSpec((1, tm, D_MODEL), lambda b, i, c: (b, i, 0)),
            pl.BlockSpec((1, tm, POOL_WIDTH), lambda b, i, c: (b, i, 0)),
            pl.BlockSpec((1, tm, n_mla), lambda b, i, c: (b, i, 0)),
            pl.BlockSpec((1, tm, X_WIDTH), lambda b, i, c: (b, i, 0)),
            _const_spec((D_MODEL, D_MODEL)),
            _const_spec((1, D_MODEL)),
            pl.BlockSpec((D_MODEL, TF), lambda b, i, c: (0, c)),
            pl.BlockSpec((D_MODEL, TF), lambda b, i, c: (0, c)),
            pl.BlockSpec((CONV_WIDTH, TF), lambda b, i, c: (0, c)),
            pl.BlockSpec((1, TF), lambda b, i, c: (0, c)),
            pl.BlockSpec((TF, D_MODEL), lambda b, i, c: (c, 0)),
        ],
        out_specs=pl.BlockSpec((1, tm, D_MODEL), lambda b, i, c: (b, i, 0)),
        out_shape=jax.ShapeDtypeStruct((B, S, D_MODEL), F32),
        scratch_shapes=[
            pltpu.VMEM((tm, D_MODEL), BF16),
            pltpu.VMEM((SUBLANES + tm, TF), F32),
            pltpu.VMEM((n_ff, SUBLANES, TF), F32),
        ],
        compiler_params=pltpu.CompilerParams(
            dimension_semantics=("arbitrary", "arbitrary", "arbitrary"),
            vmem_limit_bytes=VMEM_LIMIT),
        name="out_ffn",
    )(x, yp, ya, ym, w_o, g_ffn, w_gate, w_up, conv_w, conv_b, w_down)


def _dup_rope_cols(w):
    x1, x2 = w[..., :ROPE_HALF], w[..., ROPE_HALF:]
    return jnp.concatenate([x1, x2, x2, x1], axis=-1)


def _rope_tables(S):
    inv_freq = 1.0 / (ROPE_THETA ** (jnp.arange(ROPE_HALF, dtype=F32) / ROPE_HALF))
    ang = jnp.arange(S).astype(F32)[:, None] * inv_freq[None, :]
    cos, sin = jnp.cos(ang), jnp.sin(ang)
    zeros = jnp.zeros((S, 2 * ROPE_HALF), F32)
    return (jnp.concatenate([cos, cos, zeros], axis=-1),
            jnp.concatenate([-sin, sin, zeros], axis=-1))


def _layer(x, mem, g_mix, w_in, g_q_lat, w_q_up, g_kv_lat, w_kv_up, g_q_mla, g_k_mla,
           w_pool, pool_scale, g_mem, w_mem_kv, g_q_x, g_k_x, w_o, g_ffn,
           w_gate, w_up, conv_w, conv_b, w_down):
    S = x.shape[1]
    row = lambda v: v.reshape(1, -1)

    s0, s1, s2, s3 = _C_QLAT, _C_KVLAT, _C_KR, _C_KR + MLA_ROPE_DIM
    w_in_p = jnp.concatenate(
        [w_in[:, :s2], _dup_rope_cols(w_in[:, s2:s3]), w_in[:, s3:]], axis=1).astype(BF16)

    wq = w_q_up.reshape(Q_LORA_RANK, MLA_HEADS, MLA_QK_DIM)
    w_q_p = jnp.concatenate(
        [wq[..., :MLA_NOPE_DIM], _dup_rope_cols(wq[..., MLA_NOPE_DIM:])], axis=-1
    ).reshape(Q_LORA_RANK, MLA_HEADS * HEAD_PAD).astype(BF16)

    wkv = w_kv_up.reshape(KV_LORA_RANK, MLA_HEADS, MLA_NOPE_DIM + MLA_V_DIM)
    w_k = wkv[..., :MLA_NOPE_DIM].reshape(KV_LORA_RANK, -1).astype(BF16)
    w_vT = wkv[..., MLA_NOPE_DIM:].reshape(KV_LORA_RANK, -1).T.astype(BF16)

    cos_t, sin_t = _rope_tables(S)

    kT_mem, v_mem = _mem_kv(
        mem, row(g_mem), w_mem_kv[:, :X_WIDTH].T.astype(BF16), w_mem_kv[:, X_WIDTH:].astype(BF16),
        g_k_x.reshape(-1, 1))

    q, k, vT, yp, ym = _in_proj(
        x, cos_t, sin_t, row(g_mix), w_in_p,
        row(g_q_lat), w_q_p, row(g_q_mla[:MLA_NOPE_DIM]), row(_dup_rope_cols(g_q_mla[MLA_NOPE_DIM:])),
        row(g_kv_lat), w_k, w_vT, row(g_k_mla[:MLA_NOPE_DIM]), row(_dup_rope_cols(g_k_mla[MLA_NOPE_DIM:])),
        w_pool.astype(BF16), row(pool_scale), row(g_q_x), kT_mem, v_mem)

    ya = _mla_attn(q, k, vT)

    return _out_ffn(x, yp, ya, ym, w_o.astype(BF16), row(g_ffn),
                    w_gate.astype(BF16), w_up.astype(BF16), conv_w, row(conv_b), w_down.astype(BF16))


def kernel(x, mem, g_mix, w_in, g_q_lat, w_q_up, g_kv_lat, w_kv_up, g_q_mla, g_k_mla, w_pool, pool_scale, g_mem, w_mem_kv, g_q_x, g_k_x, w_o, g_ffn, w_gate, w_up, conv_w, conv_b, w_down):
    depth = g_mix.shape[0]
    for l in range(depth):
        x = _layer(x, mem, g_mix[l], w_in[l], g_q_lat[l], w_q_up[l], g_kv_lat[l], w_kv_up[l],
                   g_q_mla[l], g_k_mla[l], w_pool[l], pool_scale[l], g_mem[l], w_mem_kv[l],
                   g_q_x[l], g_k_x[l], w_o[l], g_ffn[l], w_gate[l], w_up[l], conv_w[l],
                   conv_b[l], w_down[l])
    return x
```

```python
import functools
import math

import jax
import jax.numpy as jnp
from jax import lax
from jax.experimental import pallas as pl
from jax.experimental.pallas import tpu as pltpu

D_MODEL = 2048
MEM_LEN = 256
POOL_WIDTH = 512
POOL_WINDOWS = (2, 4, 8, 16)
POOL_GROUP_DIM = 128
POOL_HALO = 16
MLA_HEADS = 8
MLA_V_DIM = 128
MLA_NOPE_DIM = 128
MLA_ROPE_DIM = 64
MLA_QK_DIM = MLA_NOPE_DIM + MLA_ROPE_DIM
Q_LORA_RANK = 512
KV_LORA_RANK = 256
X_HEADS = 4
X_WIDTH = 512
X_HEAD_DIM = 128
D_FF = 5632
CONV_WIDTH = 3
ROPE_THETA = 10000.0
NORM_EPS = 1e-6

LANES = 128
SUBLANES = 8
HEAD_PAD = 2 * LANES
VT_ROWS = MLA_V_DIM + 16
ROPE_HALF = MLA_ROPE_DIM // 2

_C_POOL = 0
_C_QLAT = _C_POOL + POOL_WIDTH
_C_KVLAT = _C_QLAT + Q_LORA_RANK
_C_KR = _C_KVLAT + KV_LORA_RANK
_C_MQ = _C_KR + LANES
IN_COLS_PAD = _C_MQ + X_WIDTH

TM_IN = 256
BQ = 1024
QC = 256
N_CHAIN = BQ // QC
N_BLK = 2
BK = BQ // N_BLK
ROW_CHUNK = 32
TM_FF = 512
TF = 512
VMEM_LIMIT = 56 * 1024 * 1024

BF16 = jnp.bfloat16
F32 = jnp.float32


def _dot(a, b):
    return jnp.dot(a, b, preferred_element_type=F32)


def _dot_nt(a, b):
    return lax.dot_general(a, b, (((1,), (1,)), ((), ())), preferred_element_type=F32)


def _rms_rows(xf, g):
    return xf * lax.rsqrt(jnp.mean(xf * xf, axis=-1, keepdims=True) + NORM_EPS) * g


def _rope_group(grp, cos_t, sin_t):
    return grp * cos_t + pltpu.roll(grp, 2 * ROPE_HALF, 1) * sin_t


def _const_spec(shape):
    nd = len(shape)
    return pl.BlockSpec(shape, lambda *_: (0,) * nd)


def _mem_kv_kernel(mem_ref, g_mem_ref, wkT_ref, wv_ref, gk_ref, kT_ref, v_ref):
    h = _rms_rows(mem_ref[0], g_mem_ref[...]).astype(BF16)
    kT = _dot_nt(wkT_ref[...], h)
    for hd in range(X_HEADS):
        rows = slice(hd * X_HEAD_DIM, (hd + 1) * X_HEAD_DIM)
        kh = kT[rows, :]
        r = lax.rsqrt(jnp.mean(kh * kh, axis=0, keepdims=True) + NORM_EPS)
        kT_ref[0, rows, :] = (kh * r * gk_ref[...]).astype(BF16)
    v_ref[0] = _dot(h, wv_ref[...]).astype(BF16)


def _mem_kv(mem, g_mem, wkT, wv, gk_col):
    B = mem.shape[0]
    return pl.pallas_call(
        _mem_kv_kernel,
        grid=(B,),
        in_specs=[
            pl.BlockSpec((1, MEM_LEN, D_MODEL), lambda b: (b, 0, 0)),
            _const_spec((1, D_MODEL)),
            _const_spec((X_WIDTH, D_MODEL)),
            _const_spec((D_MODEL, X_WIDTH)),
            _const_spec((X_HEAD_DIM, 1)),
        ],
        out_specs=[
            pl.BlockSpec((1, X_WIDTH, MEM_LEN), lambda b: (b, 0, 0)),
            pl.BlockSpec((1, MEM_LEN, X_WIDTH), lambda b: (b, 0, 0)),
        ],
        out_shape=[
            jax.ShapeDtypeStruct((B, X_WIDTH, MEM_LEN), BF16),
            jax.ShapeDtypeStruct((B, MEM_LEN, X_WIDTH), BF16),
        ],
        compiler_params=pltpu.CompilerParams(
            dimension_semantics=("arbitrary",), vmem_limit_bytes=VMEM_LIMIT),
        name="mem_kv",
    )(mem, g_mem, wkT, wv, gk_col)


def _in_proj_kernel(x_ref, cos_ref, sin_ref, cosT_ref, sinT_ref, g_mix_ref, w_in_ref,
                    g_ql_ref, w_qT_ref, gq_col_ref,
                    g_kvl_ref, w_k_ref, w_vT_ref, gk_nope_ref, gk_grp_ref,
                    w_pool_ref, pool_scale_ref,
                    g_qx_ref, kT_mem_ref, v_mem_ref,
                    qT_ref, k_ref, vT_ref, yp_ref, ym_ref,
                    zp_ref):
    i = pl.program_id(1)
    tm = x_ref.shape[1]
    cos_t = cos_ref[...]
    sin_t = sin_ref[...]

    @pl.when(i == 0)
    def _():
        zp_ref[0:POOL_HALO, :] = jnp.zeros((POOL_HALO, POOL_WIDTH), F32)

    @pl.when(i > 0)
    def _():
        zp_ref[0:POOL_HALO, :] = zp_ref[tm:tm + POOL_HALO, :]

    h = _rms_rows(x_ref[0], g_mix_ref[...]).astype(BF16)
    z = _dot(h, w_in_ref[...])
    z_pool = z[:, _C_POOL:_C_POOL + POOL_WIDTH]
    z_q = z[:, _C_QLAT:_C_QLAT + Q_LORA_RANK]
    z_kv = z[:, _C_KVLAT:_C_KVLAT + KV_LORA_RANK]
    kr = z[:, _C_KR:_C_KR + LANES]
    z_mq = z[:, _C_MQ:_C_MQ + X_WIDTH]

    zp_ref[POOL_HALO:POOL_HALO + tm, :] = z_pool
    t = i * tm + lax.broadcasted_iota(jnp.int32, (tm, 1), 0)
    for g, w in enumerate(POOL_WINDOWS):
        cols = slice(g * POOL_GROUP_DIM, (g + 1) * POOL_GROUP_DIM)
        s = z_pool[:, cols]
        for j in range(1, w):
            s = s + zp_ref[POOL_HALO - j:POOL_HALO - j + tm, cols]
        cnt = jnp.minimum(t + 1, w).astype(F32)
        d = s / cnt - z_pool[:, cols]
        y = _dot(d.astype(BF16), w_pool_ref[g]) * pool_scale_ref[:, cols]
        yp_ref[0, :, cols] = y.astype(BF16)

    q_scale = math.log2(math.e) / math.sqrt(MLA_QK_DIM)
    hq = _rms_rows(z_q, g_ql_ref[...]).astype(BF16)
    qfT = _dot_nt(w_qT_ref[...], hq)
    cosT = cosT_ref[...]
    sinT = sinT_ref[...]
    for hd in range(MLA_HEADS):
        r0 = hd * HEAD_PAD
        raw = qfT[r0:r0 + MLA_QK_DIM, :]
        ssq = jnp.sum(raw * raw, axis=0, keepdims=True)
        r = lax.rsqrt(ssq * (1.0 / MLA_QK_DIM) + NORM_EPS) * q_scale
        g = raw * gq_col_ref[...]
        x1 = g[MLA_NOPE_DIM:MLA_NOPE_DIM + ROPE_HALF, :]
        x2 = g[MLA_NOPE_DIM + ROPE_HALF:MLA_QK_DIM, :]
        qT_ref[0, r0:r0 + MLA_NOPE_DIM, :] = (g[0:MLA_NOPE_DIM, :] * r).astype(BF16)
        qT_ref[0, r0 + MLA_NOPE_DIM:r0 + MLA_NOPE_DIM + ROPE_HALF, :] = (
            (x1 * cosT - x2 * sinT) * r).astype(BF16)
        qT_ref[0, r0 + MLA_NOPE_DIM + ROPE_HALF:r0 + MLA_QK_DIM, :] = (
            (x2 * cosT + x1 * sinT) * r).astype(BF16)
        qT_ref[0, r0 + MLA_QK_DIM:r0 + HEAD_PAD, :] = jnp.zeros((HEAD_PAD - MLA_QK_DIM, tm), BF16)

    hkv = _rms_rows(z_kv, g_kvl_ref[...]).astype(BF16)
    kn = _dot(hkv, w_k_ref[...])
    vT = _dot_nt(w_vT_ref[...], hkv).astype(BF16)
    for hd in range(MLA_HEADS):
        r0 = hd * VT_ROWS
        vT_ref[0, r0:r0 + MLA_V_DIM, :] = vT[hd * MLA_V_DIM:(hd + 1) * MLA_V_DIM, :]
        vT_ref[0, r0 + MLA_V_DIM:r0 + VT_ROWS, :] = jnp.ones((VT_ROWS - MLA_V_DIM, tm), BF16)
    kr_ssq = 0.5 * jnp.sum(kr * kr, axis=-1, keepdims=True)
    kr_rot = _rope_group(kr * gk_grp_ref[...], cos_t, sin_t)
    for hd in range(MLA_HEADS):
        nope = kn[:, hd * LANES:(hd + 1) * LANES]
        ssq = jnp.sum(nope * nope, axis=-1, keepdims=True) + kr_ssq
        r = lax.rsqrt(ssq * (1.0 / MLA_QK_DIM) + NORM_EPS)
        c0 = hd * HEAD_PAD
        k_ref[0, :, c0:c0 + LANES] = (nope * r * gk_nope_ref[...]).astype(BF16)
        k_ref[0, :, c0 + LANES:c0 + HEAD_PAD] = (kr_rot * r).astype(BF16)

    x_scale = 1.0 / math.sqrt(X_HEAD_DIM)
    for hd in range(X_HEADS):
        cols = slice(hd * X_HEAD_DIM, (hd + 1) * X_HEAD_DIM)
        qn = (_rms_rows(z_mq[:, cols], g_qx_ref[...]) * x_scale).astype(BF16)
        s = _dot(qn, kT_mem_ref[0, cols, :])
        m = jnp.max(s, axis=-1, keepdims=True)
        p = jnp.exp(s - m)
        l = jnp.sum(p, axis=-1, keepdims=True)
        o = _dot(p.astype(BF16), v_mem_ref[0, :, cols]) / l
        ym_ref[0, :, cols] = o.astype(BF16)


def _in_proj(x, cos_t, sin_t, cosT, sinT, g_mix, w_in_p, g_ql, w_qT, gq_col,
             g_kvl, w_k, w_vT, gk_nope, gk_grp, w_pool, pool_scale, g_qx, kT_mem, v_mem):
    B, S, _ = x.shape
    tm = TM_IN
    n_kv = MLA_HEADS * MLA_V_DIM
    in_specs = [
        pl.BlockSpec((1, tm, D_MODEL), lambda b, i: (b, i, 0)),
        pl.BlockSpec((tm, LANES), lambda b, i: (i, 0)),
        pl.BlockSpec((tm, LANES), lambda b, i: (i, 0)),
        pl.BlockSpec((ROPE_HALF, tm), lambda b, i: (0, i)),
        pl.BlockSpec((ROPE_HALF, tm), lambda b, i: (0, i)),
        _const_spec((1, D_MODEL)),
        _const_spec((D_MODEL, IN_COLS_PAD)),
        _const_spec((1, Q_LORA_RANK)),
        _const_spec((MLA_HEADS * HEAD_PAD, Q_LORA_RANK)),
        _const_spec((MLA_QK_DIM, 1)),
        _const_spec((1, KV_LORA_RANK)),
        _const_spec((KV_LORA_RANK, n_kv)),
        _const_spec((n_kv, KV_LORA_RANK)),
        _const_spec((1, LANES)),
        _const_spec((1, LANES)),
        _const_spec((len(POOL_WINDOWS), POOL_GROUP_DIM, POOL_GROUP_DIM)),
        _const_spec((1, POOL_WIDTH)),
        _const_spec((1, X_HEAD_DIM)),
        pl.BlockSpec((1, X_WIDTH, MEM_LEN), lambda b, i: (b, 0, 0)),
        pl.BlockSpec((1, MEM_LEN, X_WIDTH), lambda b, i: (b, 0, 0)),
    ]
    out_specs = [
        pl.BlockSpec((1, MLA_HEADS * HEAD_PAD, tm), lambda b, i: (b, 0, i)),
        pl.BlockSpec((1, tm, MLA_HEADS * HEAD_PAD), lambda b, i: (b, i, 0)),
        pl.BlockSpec((1, MLA_HEADS * VT_ROWS, tm), lambda b, i: (b, 0, i)),
        pl.BlockSpec((1, tm, POOL_WIDTH), lambda b, i: (b, i, 0)),
        pl.BlockSpec((1, tm, X_WIDTH), lambda b, i: (b, i, 0)),
    ]
    out_shape = [
        jax.ShapeDtypeStruct((B, MLA_HEADS * HEAD_PAD, S), BF16),
        jax.ShapeDtypeStruct((B, S, MLA_HEADS * HEAD_PAD), BF16),
        jax.ShapeDtypeStruct((B, MLA_HEADS * VT_ROWS, S), BF16),
        jax.ShapeDtypeStruct((B, S, POOL_WIDTH), BF16),
        jax.ShapeDtypeStruct((B, S, X_WIDTH), BF16),
    ]
    return pl.pallas_call(
        _in_proj_kernel,
        grid=(B, S // tm),
        in_specs=in_specs,
        out_specs=out_specs,
        out_shape=out_shape,
        scratch_shapes=[pltpu.VMEM((POOL_HALO + tm, POOL_WIDTH), F32)],
        compiler_params=pltpu.CompilerParams(
            dimension_semantics=("arbitrary", "arbitrary"), vmem_limit_bytes=VMEM_LIMIT),
        name="in_proj",
    )(x, cos_t, sin_t, cosT, sinT, g_mix, w_in_p, g_ql, w_qT, gq_col,
      g_kvl, w_k, w_vT, gk_nope, gk_grp, w_pool, pool_scale, g_qx, kT_mem, v_mem)


def _mla_attn_kernel(qT_ref, k_ref, vT_ref, o_ref, s_ref, mb_ref, p_ref, m_ref, acc_ref):
    i = pl.program_id(2)
    chains = tuple(range(N_CHAIN))

    def scores(j, start, cs):
        k = k_ref[0, pl.ds(start, BK), :]
        for c in cs:
            sT = _dot(k, qT_ref[0, :, c * QC:(c + 1) * QC])
            s_ref[j, c] = sT
            mb_ref[j, c] = jnp.max(sT, axis=0, keepdims=True)

    def mask(j, c, off):
        rr = lax.broadcasted_iota(jnp.int32, (BK, QC), 0)
        qq = lax.broadcasted_iota(jnp.int32, (BK, QC), 1)
        sT = jnp.where(rr <= qq + off, s_ref[j, c], -jnp.inf)
        s_ref[j, c] = sT
        mb_ref[j, c] = jnp.max(sT, axis=0, keepdims=True)

    def update(j, start, cs):
        for c in cs:
            m_old = m_ref[c]
            m_new = jnp.maximum(m_old, mb_ref[j, c])
            alpha = jnp.exp2(m_old - m_new)
            for r in range(0, BK, ROW_CHUNK):
                x = s_ref[j, c, r:r + ROW_CHUNK, :] - m_new
                p_ref[c, r:r + ROW_CHUNK, :] = jnp.exp2(x.astype(BF16))
            pv = _dot(vT_ref[0, :, pl.ds(start, BK)], p_ref[c])
            acc_ref[c] = acc_ref[c] * alpha + pv
            m_ref[c] = m_new

    m_ref[...] = jnp.full(m_ref.shape, -jnp.inf, F32)
    acc_ref[...] = jnp.zeros(acc_ref.shape, F32)

    scores(0, 0, chains)

    def trip(t, carry):
        b0 = pl.multiple_of(t * (2 * BK), 2 * BK)
        scores(1, b0 + BK, chains)
        update(0, b0, chains)

        @pl.when(t >= 0)
        def _():
            scores(0, b0 + 2 * BK, chains)
            update(1, b0 + BK, chains)

        return carry

    lax.fori_loop(0, i, trip, 0)

    base = pl.multiple_of(i * BQ, BQ)
    for d in range(N_BLK):
        live = tuple(c for c in chains if (c * QC) // BK >= d)
        if d + 1 < N_BLK:
            scores((d + 1) % 2, base + (d + 1) * BK, tuple(c for c in chains if (c * QC) // BK >= d + 1))
        for c in live:
            if (c * QC) // BK == d:
                mask(d % 2, c, c * QC - d * BK)
        update(d % 2, base + d * BK, live)

    for c in chains:
        o = acc_ref[c, 0:MLA_V_DIM, :] / acc_ref[c, MLA_V_DIM:MLA_V_DIM + 1, :]
        o_ref[0, c * QC:(c + 1) * QC, :] = o.T.astype(BF16)


def _mla_attn(qT, k, vT):
    B, S, _ = k.shape
    assert N_BLK == 2 and BQ % QC == 0 and BK % QC == 0 and BK % ROW_CHUNK == 0
    return pl.pallas_call(
        _mla_attn_kernel,
        grid=(B, MLA_HEADS, S // BQ),
        in_specs=[
            pl.BlockSpec((1, HEAD_PAD, BQ), lambda b, h, i: (b, h, i)),
            pl.BlockSpec((1, S, HEAD_PAD), lambda b, h, i: (b, 0, h)),
            pl.BlockSpec((1, VT_ROWS, S), lambda b, h, i: (b, h, 0)),
        ],
        out_specs=pl.BlockSpec((1, BQ, MLA_V_DIM), lambda b, h, i: (b, i, h)),
        out_shape=jax.ShapeDtypeStruct((B, S, MLA_HEADS * MLA_V_DIM), BF16),
        scratch_shapes=[
            pltpu.VMEM((2, N_CHAIN, BK, QC), F32),
            pltpu.VMEM((2, N_CHAIN, 1, QC), F32),
            pltpu.VMEM((N_CHAIN, BK, QC), BF16),
            pltpu.VMEM((N_CHAIN, 1, QC), F32),
            pltpu.VMEM((N_CHAIN, VT_ROWS, QC), F32),
        ],
        compiler_params=pltpu.CompilerParams(
            dimension_semantics=("arbitrary", "arbitrary", "arbitrary"),
            vmem_limit_bytes=VMEM_LIMIT),
        name="mla_attn",
    )(qT, k, vT)


def _out_ffn_kernel(x_ref, yp_ref, ya_ref, ym_ref, w_o_ref, g_ffn_ref,
                    w_gate_ref, w_up_ref, conv_w_ref, conv_b_ref, w_down_ref,
                    o_ref, h2_ref, gbuf_ref, carry_ref):
    i = pl.program_id(1)
    c = pl.program_id(2)
    tm = x_ref.shape[1]
    n_mla = MLA_HEADS * MLA_V_DIM

    @pl.when(c == 0)
    def _():
        x1 = (x_ref[0]
              + _dot(yp_ref[0], w_o_ref[0:POOL_WIDTH, :])
              + _dot(ya_ref[0], w_o_ref[POOL_WIDTH:POOL_WIDTH + n_mla, :])
              + _dot(ym_ref[0], w_o_ref[POOL_WIDTH + n_mla:, :]))
        o_ref[0] = x1
        h2_ref[...] = _rms_rows(x1, g_ffn_ref[...]).astype(BF16)

        @pl.when(i == 0)
        def _():
            carry_ref[...] = jnp.zeros(carry_ref.shape, F32)

    h2 = h2_ref[...]
    g = _dot(h2, w_gate_ref[...])
    u = _dot(h2, w_up_ref[...])

    gbuf_ref[0:SUBLANES, :] = carry_ref[c]
    gbuf_ref[SUBLANES:SUBLANES + tm, :] = g
    carry_ref[c] = g[tm - SUBLANES:tm, :]
    gc = conv_b_ref[...] + conv_w_ref[CONV_WIDTH - 1:CONV_WIDTH, :] * g
    for j in range(CONV_WIDTH - 1):
        lag = CONV_WIDTH - 1 - j
        gc = gc + conv_w_ref[j:j + 1, :] * gbuf_ref[SUBLANES - lag:SUBLANES - lag + tm, :]
    act = (gc / (1.0 + jnp.exp(-gc))) * u
    o_ref[0] += _dot(act.astype(BF16), w_down_ref[...])


def _out_ffn(x, yp, ya, ym, w_o, g_ffn, w_gate, w_up, conv_w, conv_b, w_down):
    B, S, _ = x.shape
    tm = TM_FF
    n_mla = MLA_HEADS * MLA_V_DIM
    n_ff = D_FF // TF
    assert n_ff * TF == D_FF
    return pl.pallas_call(
        _out_ffn_kernel,
        grid=(B, S // tm, n_ff),
        in_specs=[
            pl.BlockSpec((1, tm, D_MODEL), lambda b, i, c: (b, i, 0)),
            pl.BlockSpec((1, tm, POOL_WIDTH), lambda b, i, c: (b, i, 0)),
            pl.BlockSpec((1, tm, n_mla), lambda b, i, c: (b, i, 0)),
            pl.BlockSpec((1, tm, X_WIDTH), lambda b, i, c: (b, i, 0)),
            _const_spec((D_MODEL, D_MODEL)),
            _const_spec((1, D_MODEL)),
            pl.BlockSpec((D_MODEL, TF), lambda b, i, c: (0, c)),
            pl.BlockSpec((D_MODEL, TF), lambda b, i, c: (0, c)),
            pl.BlockSpec((CONV_WIDTH, TF), lambda b, i, c: (0, c)),
            pl.BlockSpec((1, TF), lambda b, i, c: (0, c)),
            pl.BlockSpec((TF, D_MODEL), lambda b, i, c: (c, 0)),
        ],
        out_specs=pl.BlockSpec((1, tm, D_MODEL), lambda b, i, c: (b, i, 0)),
        out_shape=jax.ShapeDtypeStruct((B, S, D_MODEL), F32),
        scratch_shapes=[
            pltpu.VMEM((tm, D_MODEL), BF16),
            pltpu.VMEM((SUBLANES + tm, TF), F32),
            pltpu.VMEM((n_ff, SUBLANES, TF), F32),
        ],
        compiler_params=pltpu.CompilerParams(
            dimension_semantics=("arbitrary", "arbitrary", "arbitrary"),
            vmem_limit_bytes=VMEM_LIMIT),
        name="out_ffn",
    )(x, yp, ya, ym, w_o, g_ffn, w_gate, w_up, conv_w, conv_b, w_down)


def _dup_rope_cols(w):
    x1, x2 = w[..., :ROPE_HALF], w[..., ROPE_HALF:]
    return jnp.concatenate([x1, x2, x2, x1], axis=-1)


def _rope_tables(S):
    inv_freq = 1.0 / (ROPE_THETA ** (jnp.arange(ROPE_HALF, dtype=F32) / ROPE_HALF))
    ang = jnp.arange(S).astype(F32)[:, None] * inv_freq[None, :]
    cos, sin = jnp.cos(ang), jnp.sin(ang)
    zeros = jnp.zeros((S, 2 * ROPE_HALF), F32)
    return (jnp.concatenate([cos, cos, zeros], axis=-1),
            jnp.concatenate([-sin, sin, zeros], axis=-1), cos.T, sin.T)


def _layer(x, mem, g_mix, w_in, g_q_lat, w_q_up, g_kv_lat, w_kv_up, g_q_mla, g_k_mla,
           w_pool, pool_scale, g_mem, w_mem_kv, g_q_x, g_k_x, w_o, g_ffn,
           w_gate, w_up, conv_w, conv_b, w_down):
    S = x.shape[1]
    row = lambda v: v.reshape(1, -1)

    s0, s1, s2, s3 = _C_QLAT, _C_KVLAT, _C_KR, _C_KR + MLA_ROPE_DIM
    w_in_p = jnp.concatenate(
        [w_in[:, :s2], _dup_rope_cols(w_in[:, s2:s3]), w_in[:, s3:]], axis=1).astype(BF16)

    wq = w_q_up.reshape(Q_LORA_RANK, MLA_HEADS, MLA_QK_DIM)
    w_qT = jnp.pad(wq, ((0, 0), (0, 0), (0, HEAD_PAD - MLA_QK_DIM))).reshape(
        Q_LORA_RANK, MLA_HEADS * HEAD_PAD).T.astype(BF16)

    wkv = w_kv_up.reshape(KV_LORA_RANK, MLA_HEADS, MLA_NOPE_DIM + MLA_V_DIM)
    w_k = wkv[..., :MLA_NOPE_DIM].reshape(KV_LORA_RANK, -1).astype(BF16)
    w_vT = wkv[..., MLA_NOPE_DIM:].reshape(KV_LORA_RANK, -1).T.astype(BF16)

    cos_t, sin_t, cosT, sinT = _rope_tables(S)

    kT_mem, v_mem = _mem_kv(
        mem, row(g_mem), w_mem_kv[:, :X_WIDTH].T.astype(BF16), w_mem_kv[:, X_WIDTH:].astype(BF16),
        g_k_x.reshape(-1, 1))

    qT, k, vT, yp, ym = _in_proj(
        x, cos_t, sin_t, cosT, sinT, row(g_mix), w_in_p,
        row(g_q_lat), w_qT, g_q_mla.reshape(-1, 1),
        row(g_kv_lat), w_k, w_vT, row(g_k_mla[:MLA_NOPE_DIM]), row(_dup_rope_cols(g_k_mla[MLA_NOPE_DIM:])),
        w_pool.astype(BF16), row(pool_scale), row(g_q_x), kT_mem, v_mem)

    ya = _mla_attn(qT, k, vT)

    return _out_ffn(x, yp, ya, ym, w_o.astype(BF16), row(g_ffn),
                    w_gate.astype(BF16), w_up.astype(BF16), conv_w, row(conv_b), w_down.astype(BF16))


def kernel(x, mem, g_mix, w_in, g_q_lat, w_q_up, g_kv_lat, w_kv_up, g_q_mla, g_k_mla, w_pool, pool_scale, g_mem, w_mem_kv, g_q_x, g_k_x, w_o, g_ffn, w_gate, w_up, conv_w, conv_b, w_down):
    depth = g_mix.shape[0]
    for l in range(depth):
        x = _layer(x, mem, g_mix[l], w_in[l], g_q_lat[l], w_q_up[l], g_kv_lat[l], w_kv_up[l],
                   g_q_mla[l], g_k_mla[l], w_pool[l], pool_scale[l], g_mem[l], w_mem_kv[l],
                   g_q_x[l], g_k_x[l], w_o[l], g_ffn[l], w_gate[l], w_up[l], conv_w[l],
                   conv_b[l], w_down[l])
    return x
```

```python
import functools
import math

import jax
import jax.numpy as jnp
from jax import lax
from jax.experimental import pallas as pl
from jax.experimental.pallas import tpu as pltpu

D_MODEL = 2048
MEM_LEN = 256
POOL_WIDTH = 512
POOL_WINDOWS = (2, 4, 8, 16)
POOL_GROUP_DIM = 128
POOL_HALO = 16
MLA_HEADS = 8
MLA_V_DIM = 128
MLA_NOPE_DIM = 128
MLA_ROPE_DIM = 64
MLA_QK_DIM = MLA_NOPE_DIM + MLA_ROPE_DIM
Q_LORA_RANK = 512
KV_LORA_RANK = 256
X_HEADS = 4
X_WIDTH = 512
X_HEAD_DIM = 128
D_FF = 5632
CONV_WIDTH = 3
ROPE_THETA = 10000.0
NORM_EPS = 1e-6

LANES = 128
SUBLANES = 8
HEAD_PAD = 2 * LANES
VT_ROWS = MLA_V_DIM + 16
ROPE_HALF = MLA_ROPE_DIM // 2

_C_POOL = 0
_C_QLAT = _C_POOL + POOL_WIDTH
_C_KVLAT = _C_QLAT + Q_LORA_RANK
_C_KR = _C_KVLAT + KV_LORA_RANK
_C_MQ = _C_KR + LANES
IN_COLS_PAD = _C_MQ + X_WIDTH

TM_IN = 512
BQ = 1024
QC = 256
N_CHAIN = BQ // QC
N_BLK = 2
BK = BQ // N_BLK
ROW_CHUNK = 32
TM_FF = 512
TF = 512
VMEM_LIMIT = 56 * 1024 * 1024

BF16 = jnp.bfloat16
F32 = jnp.float32


def _dot(a, b):
    return jnp.dot(a, b, preferred_element_type=F32)


def _dot_nt(a, b):
    return lax.dot_general(a, b, (((1,), (1,)), ((), ())), preferred_element_type=F32)


def _rms_rows(xf, g):
    return xf * lax.rsqrt(jnp.mean(xf * xf, axis=-1, keepdims=True) + NORM_EPS) * g


def _rope_group(grp, cos_t, sin_t):
    return grp * cos_t + pltpu.roll(grp, 2 * ROPE_HALF, 1) * sin_t


def _const_spec(shape):
    nd = len(shape)
    return pl.BlockSpec(shape, lambda *_: (0,) * nd, pipeline_mode=pl.Buffered(1))


def _mem_kv_kernel(mem_ref, g_mem_ref, wkT_ref, wv_ref, gk_ref, kT_ref, v_ref):
    h = _rms_rows(mem_ref[0], g_mem_ref[...]).astype(BF16)
    kT = _dot_nt(wkT_ref[...], h)
    for hd in range(X_HEADS):
        rows = slice(hd * X_HEAD_DIM, (hd + 1) * X_HEAD_DIM)
        kh = kT[rows, :]
        r = lax.rsqrt(jnp.mean(kh * kh, axis=0, keepdims=True) + NORM_EPS)
        kT_ref[0, rows, :] = (kh * r * gk_ref[...]).astype(BF16)
    v_ref[0] = _dot(h, wv_ref[...]).astype(BF16)


def _mem_kv(mem, g_mem, wkT, wv, gk_col):
    B = mem.shape[0]
    return pl.pallas_call(
        _mem_kv_kernel,
        grid=(B,),
        in_specs=[
            pl.BlockSpec((1, MEM_LEN, D_MODEL), lambda b: (b, 0, 0)),
            _const_spec((1, D_MODEL)),
            _const_spec((X_WIDTH, D_MODEL)),
            _const_spec((D_MODEL, X_WIDTH)),
            _const_spec((X_HEAD_DIM, 1)),
        ],
        out_specs=[
            pl.BlockSpec((1, X_WIDTH, MEM_LEN), lambda b: (b, 0, 0)),
            pl.BlockSpec((1, MEM_LEN, X_WIDTH), lambda b: (b, 0, 0)),
        ],
        out_shape=[
            jax.ShapeDtypeStruct((B, X_WIDTH, MEM_LEN), BF16),
            jax.ShapeDtypeStruct((B, MEM_LEN, X_WIDTH), BF16),
        ],
        compiler_params=pltpu.CompilerParams(
            dimension_semantics=("arbitrary",), vmem_limit_bytes=VMEM_LIMIT),
        name="mem_kv",
    )(mem, g_mem, wkT, wv, gk_col)


def _in_proj_kernel(x_ref, cos_ref, sin_ref, cosT_ref, sinT_ref, g_mix_ref, w_in_ref,
                    g_ql_ref, w_qT_ref, gq_col_ref,
                    g_kvl_ref, w_k_ref, w_vT_ref, gk_nope_ref, gk_grp_ref,
                    w_pool_ref, pool_scale_ref,
                    g_qx_ref, kT_mem_ref, v_mem_ref,
                    qT_ref, k_ref, vT_ref, yp_ref, ym_ref,
                    zp_ref):
    i = pl.program_id(1)
    tm = x_ref.shape[1]
    cos_t = cos_ref[...]
    sin_t = sin_ref[...]

    @pl.when(i == 0)
    def _():
        zp_ref[0:POOL_HALO, :] = jnp.zeros((POOL_HALO, POOL_WIDTH), F32)

    @pl.when(i > 0)
    def _():
        zp_ref[0:POOL_HALO, :] = zp_ref[tm:tm + POOL_HALO, :]

    h = _rms_rows(x_ref[0], g_mix_ref[...]).astype(BF16)
    z = _dot(h, w_in_ref[...])
    z_pool = z[:, _C_POOL:_C_POOL + POOL_WIDTH]
    z_q = z[:, _C_QLAT:_C_QLAT + Q_LORA_RANK]
    z_kv = z[:, _C_KVLAT:_C_KVLAT + KV_LORA_RANK]
    kr = z[:, _C_KR:_C_KR + LANES]
    z_mq = z[:, _C_MQ:_C_MQ + X_WIDTH]

    zp_ref[POOL_HALO:POOL_HALO + tm, :] = z_pool
    t = i * tm + lax.broadcasted_iota(jnp.int32, (tm, 1), 0)
    for g, w in enumerate(POOL_WINDOWS):
        cols = slice(g * POOL_GROUP_DIM, (g + 1) * POOL_GROUP_DIM)
        s = z_pool[:, cols]
        for j in range(1, w):
            s = s + zp_ref[POOL_HALO - j:POOL_HALO - j + tm, cols]
        cnt = jnp.minimum(t + 1, w).astype(F32)
        d = s / cnt - z_pool[:, cols]
        y = _dot(d.astype(BF16), w_pool_ref[g]) * pool_scale_ref[:, cols]
        yp_ref[0, :, cols] = y.astype(BF16)

    q_scale = math.log2(math.e) / math.sqrt(MLA_QK_DIM)
    hq = _rms_rows(z_q, g_ql_ref[...]).astype(BF16)
    qfT = _dot_nt(w_qT_ref[...], hq)
    cosT = cosT_ref[...]
    sinT = sinT_ref[...]
    for hd in range(MLA_HEADS):
        r0 = hd * HEAD_PAD
        raw = qfT[r0:r0 + MLA_QK_DIM, :]
        ssq = jnp.sum(raw * raw, axis=0, keepdims=True)
        r = lax.rsqrt(ssq * (1.0 / MLA_QK_DIM) + NORM_EPS) * q_scale
        g = raw * gq_col_ref[...]
        x1 = g[MLA_NOPE_DIM:MLA_NOPE_DIM + ROPE_HALF, :]
        x2 = g[MLA_NOPE_DIM + ROPE_HALF:MLA_QK_DIM, :]
        qT_ref[0, r0:r0 + MLA_NOPE_DIM, :] = (g[0:MLA_NOPE_DIM, :] * r).astype(BF16)
        qT_ref[0, r0 + MLA_NOPE_DIM:r0 + MLA_NOPE_DIM + ROPE_HALF, :] = (
            (x1 * cosT - x2 * sinT) * r).astype(BF16)
        qT_ref[0, r0 + MLA_NOPE_DIM + ROPE_HALF:r0 + MLA_QK_DIM, :] = (
            (x2 * cosT + x1 * sinT) * r).astype(BF16)
        qT_ref[0, r0 + MLA_QK_DIM:r0 + HEAD_PAD, :] = jnp.zeros((HEAD_PAD - MLA_QK_DIM, tm), BF16)

    hkv = _rms_rows(z_kv, g_kvl_ref[...]).astype(BF16)
    kn = _dot(hkv, w_k_ref[...])
    vT = _dot_nt(w_vT_ref[...], hkv).astype(BF16)
    for hd in range(MLA_HEADS):
        r0 = hd * VT_ROWS
        vT_ref[0, r0:r0 + MLA_V_DIM, :] = vT[hd * MLA_V_DIM:(hd + 1) * MLA_V_DIM, :]
        vT_ref[0, r0 + MLA_V_DIM:r0 + VT_ROWS, :] = jnp.ones((VT_ROWS - MLA_V_DIM, tm), BF16)
    kr_ssq = 0.5 * jnp.sum(kr * kr, axis=-1, keepdims=True)
    kr_rot = _rope_group(kr * gk_grp_ref[...], cos_t, sin_t)
    for hd in range(MLA_HEADS):
        nope = kn[:, hd * LANES:(hd + 1) * LANES]
        ssq = jnp.sum(nope * nope, axis=-1, keepdims=True) + kr_ssq
        r = lax.rsqrt(ssq * (1.0 / MLA_QK_DIM) + NORM_EPS)
        c0 = hd * HEAD_PAD
        k_ref[0, :, c0:c0 + LANES] = (nope * r * gk_nope_ref[...]).astype(BF16)
        k_ref[0, :, c0 + LANES:c0 + HEAD_PAD] = (kr_rot * r).astype(BF16)

    x_scale = 1.0 / math.sqrt(X_HEAD_DIM)
    for hd in range(X_HEADS):
        cols = slice(hd * X_HEAD_DIM, (hd + 1) * X_HEAD_DIM)
        qn = (_rms_rows(z_mq[:, cols], g_qx_ref[...]) * x_scale).astype(BF16)
        s = _dot(qn, kT_mem_ref[0, cols, :])
        m = jnp.max(s, axis=-1, keepdims=True)
        p = jnp.exp(s - m)
        l = jnp.sum(p, axis=-1, keepdims=True)
        o = _dot(p.astype(BF16), v_mem_ref[0, :, cols]) / l
        ym_ref[0, :, cols] = o.astype(BF16)


def _in_proj(x, cos_t, sin_t, cosT, sinT, g_mix, w_in_p, g_ql, w_qT, gq_col,
             g_kvl, w_k, w_vT, gk_nope, gk_grp, w_pool, pool_scale, g_qx, kT_mem, v_mem):
    B, S, _ = x.shape
    tm = TM_IN
    n_kv = MLA_HEADS * MLA_V_DIM
    in_specs = [
        pl.BlockSpec((1, tm, D_MODEL), lambda b, i: (b, i, 0)),
        pl.BlockSpec((tm, LANES), lambda b, i: (i, 0)),
        pl.BlockSpec((tm, LANES), lambda b, i: (i, 0)),
        pl.BlockSpec((ROPE_HALF, tm), lambda b, i: (0, i)),
        pl.BlockSpec((ROPE_HALF, tm), lambda b, i: (0, i)),
        _const_spec((1, D_MODEL)),
        _const_spec((D_MODEL, IN_COLS_PAD)),
        _const_spec((1, Q_LORA_RANK)),
        _const_spec((MLA_HEADS * HEAD_PAD, Q_LORA_RANK)),
        _const_spec((MLA_QK_DIM, 1)),
        _const_spec((1, KV_LORA_RANK)),
        _const_spec((KV_LORA_RANK, n_kv)),
        _const_spec((n_kv, KV_LORA_RANK)),
        _const_spec((1, LANES)),
        _const_spec((1, LANES)),
        _const_spec((len(POOL_WINDOWS), POOL_GROUP_DIM, POOL_GROUP_DIM)),
        _const_spec((1, POOL_WIDTH)),
        _const_spec((1, X_HEAD_DIM)),
        pl.BlockSpec((1, X_WIDTH, MEM_LEN), lambda b, i: (b, 0, 0)),
        pl.BlockSpec((1, MEM_LEN, X_WIDTH), lambda b, i: (b, 0, 0)),
    ]
    out_specs = [
        pl.BlockSpec((1, MLA_HEADS * HEAD_PAD, tm), lambda b, i: (b, 0, i)),
        pl.BlockSpec((1, tm, MLA_HEADS * HEAD_PAD), lambda b, i: (b, i, 0)),
        pl.BlockSpec((1, MLA_HEADS * VT_ROWS, tm), lambda b, i: (b, 0, i)),
        pl.BlockSpec((1, tm, POOL_WIDTH), lambda b, i: (b, i, 0)),
        pl.BlockSpec((1, tm, X_WIDTH), lambda b, i: (b, i, 0)),
    ]
    out_shape = [
        jax.ShapeDtypeStruct((B, MLA_HEADS * HEAD_PAD, S), BF16),
        jax.ShapeDtypeStruct((B, S, MLA_HEADS * HEAD_PAD), BF16),
        jax.ShapeDtypeStruct((B, MLA_HEADS * VT_ROWS, S), BF16),
        jax.ShapeDtypeStruct((B, S, POOL_WIDTH), BF16),
        jax.ShapeDtypeStruct((B, S, X_WIDTH), BF16),
    ]
    return pl.pallas_call(
        _in_proj_kernel,
        grid=(B, S // tm),
        in_specs=in_specs,
        out_specs=out_specs,
        out_shape=out_shape,
        scratch_shapes=[pltpu.VMEM((POOL_HALO + tm, POOL_WIDTH), F32)],
        compiler_params=pltpu.CompilerParams(
            dimension_semantics=("arbitrary", "arbitrary"), vmem_limit_bytes=VMEM_LIMIT),
        name="in_proj",
    )(x, cos_t, sin_t, cosT, sinT, g_mix, w_in_p, g_ql, w_qT, gq_col,
      g_kvl, w_k, w_vT, gk_nope, gk_grp, w_pool, pool_scale, g_qx, kT_mem, v_mem)


def _mla_attn_kernel(qT_ref, k_ref, vT_ref, o_ref, s_ref, mb_ref, p_ref, m_ref, acc_ref):
    n_q = k_ref.shape[1] // BQ
    chains = tuple(range(N_CHAIN))

    def scores(j, start, q0, cs):
        k = k_ref[0, pl.ds(start, BK), :]
        for c in cs:
            qT = qT_ref[0, :, pl.ds(pl.multiple_of(q0 + c * QC, QC), QC)]
            sT = _dot(k, qT)
            s_ref[j, c] = sT
            mb_ref[j, c] = jnp.max(sT, axis=0, keepdims=True)

    def mask(j, c, off):
        rr = lax.broadcasted_iota(jnp.int32, (BK, QC), 0)
        qq = lax.broadcasted_iota(jnp.int32, (BK, QC), 1)
        sT = jnp.where(rr <= qq + off, s_ref[j, c], -jnp.inf)
        s_ref[j, c] = sT
        mb_ref[j, c] = jnp.max(sT, axis=0, keepdims=True)

    def update(j, start, cs):
        for c in cs:
            m_old = m_ref[c]
            m_new = jnp.maximum(m_old, mb_ref[j, c])
            alpha = jnp.exp2(m_old - m_new)
            for r in range(0, BK, ROW_CHUNK):
                x = s_ref[j, c, r:r + ROW_CHUNK, :] - m_new
                p_ref[c, r:r + ROW_CHUNK, :] = jnp.exp2(x.astype(BF16))
            pv = _dot(vT_ref[0, :, pl.ds(start, BK)], p_ref[c])
            acc_ref[c] = acc_ref[c] * alpha + pv
            m_ref[c] = m_new

    in_blk = lambda d: tuple(c for c in chains if (c * QC) // BK == d)
    from_blk = lambda d: tuple(c for c in chains if (c * QC) // BK >= d)

    scores(0, 0, 0, chains)

    def qblock(i, carry):
        q0 = pl.multiple_of(i * BQ, BQ)
        m_ref[...] = jnp.full(m_ref.shape, -jnp.inf, F32)
        acc_ref[...] = jnp.zeros(acc_ref.shape, F32)

        def trip(t, carry):
            b0 = pl.multiple_of(t * (2 * BK), 2 * BK)
            scores(1, b0 + BK, q0, chains)
            update(0, b0, chains)

            @pl.when(t >= 0)
            def _():
                scores(0, b0 + 2 * BK, q0, chains)
                update(1, b0 + BK, chains)

            return carry

        lax.fori_loop(0, i, trip, 0)

        scores(1, q0 + BK, q0, from_blk(1))
        for c in in_blk(0):
            mask(0, c, c * QC)
        update(0, q0, chains)

        @pl.when(i >= 0)
        def _():
            q_next = pl.multiple_of(jnp.minimum(i + 1, n_q - 1) * BQ, BQ)
            scores(0, 0, q_next, chains)
            for c in in_blk(1):
                mask(1, c, c * QC - BK)
            update(1, q0 + BK, from_blk(1))
            for c in chains:
                o = acc_ref[c, 0:MLA_V_DIM, :] / acc_ref[c, MLA_V_DIM:MLA_V_DIM + 1, :]
                o_ref[0, pl.ds(pl.multiple_of(q0 + c * QC, QC), QC), :] = o.T.astype(BF16)

        return carry

    lax.fori_loop(0, n_q, qblock, 0)


def _mla_attn(qT, k, vT):
    B, S, _ = k.shape
    assert N_BLK == 2 and BQ % QC == 0 and BK % QC == 0 and BK % ROW_CHUNK == 0 and S % BQ == 0
    return pl.pallas_call(
        _mla_attn_kernel,
        grid=(B, MLA_HEADS),
        in_specs=[
            pl.BlockSpec((1, HEAD_PAD, S), lambda b, h: (b, h, 0)),
            pl.BlockSpec((1, S, HEAD_PAD), lambda b, h: (b, 0, h)),
            pl.BlockSpec((1, VT_ROWS, S), lambda b, h: (b, h, 0)),
        ],
        out_specs=pl.BlockSpec((1, S, MLA_V_DIM), lambda b, h: (b, 0, h)),
        out_shape=jax.ShapeDtypeStruct((B, S, MLA_HEADS * MLA_V_DIM), BF16),
        scratch_shapes=[
            pltpu.VMEM((2, N_CHAIN, BK, QC), F32),
            pltpu.VMEM((2, N_CHAIN, 1, QC), F32),
            pltpu.VMEM((N_CHAIN, BK, QC), BF16),
            pltpu.VMEM((N_CHAIN, 1, QC), F32),
            pltpu.VMEM((N_CHAIN, VT_ROWS, QC), F32),
        ],
        compiler_params=pltpu.CompilerParams(
            dimension_semantics=("arbitrary", "arbitrary"),
            vmem_limit_bytes=VMEM_LIMIT),
        name="mla_attn",
    )(qT, k, vT)


def _out_ffn_kernel(x_ref, yp_ref, ya_ref, ym_ref, w_o_ref, g_ffn_ref,
                    w_gate_ref, w_up_ref, conv_w_ref, conv_b_ref, w_down_ref,
                    o_ref, h2_ref, gbuf_ref, carry_ref):
    i = pl.program_id(1)
    c = pl.program_id(2)
    tm = x_ref.shape[1]
    n_mla = MLA_HEADS * MLA_V_DIM

    @pl.when(c == 0)
    def _():
        x1 = (x_ref[0]
              + _dot(yp_ref[0], w_o_ref[0:POOL_WIDTH, :])
              + _dot(ya_ref[0], w_o_ref[POOL_WIDTH:POOL_WIDTH + n_mla, :])
              + _dot(ym_ref[0], w_o_ref[POOL_WIDTH + n_mla:, :]))
        o_ref[0] = x1
        h2_ref[...] = _rms_rows(x1, g_ffn_ref[...]).astype(BF16)

        @pl.when(i == 0)
        def _():
            carry_ref[...] = jnp.zeros(carry_ref.shape, F32)

    h2 = h2_ref[...]
    g = _dot(h2, w_gate_ref[...])
    u = _dot(h2, w_up_ref[...])

    gbuf_ref[0:SUBLANES, :] = carry_ref[c]
    gbuf_ref[SUBLANES:SUBLANES + tm, :] = g
    carry_ref[c] = g[tm - SUBLANES:tm, :]
    gc = conv_b_ref[...] + conv_w_ref[CONV_WIDTH - 1:CONV_WIDTH, :] * g
    for j in range(CONV_WIDTH - 1):
        lag = CONV_WIDTH - 1 - j
        gc = gc + conv_w_ref[j:j + 1, :] * gbuf_ref[SUBLANES - lag:SUBLANES - lag + tm, :]
    act = (gc / (1.0 + jnp.exp(-gc))) * u
    o_ref[0] += _dot(act.astype(BF16), w_down_ref[...])


def _out_ffn(x, yp, ya, ym, w_o, g_ffn, w_gate, w_up, conv_w, conv_b, w_down):
    B, S, _ = x.shape
    tm = TM_FF
    n_mla = MLA_HEADS * MLA_V_DIM
    n_ff = D_FF // TF
    assert n_ff * TF == D_FF
    n_tiles = S // tm

    def tile_in(b, i, c):
        return (b, jnp.minimum(i + jnp.minimum(c, 1), n_tiles - 1), 0)

    return pl.pallas_call(
        _out_ffn_kernel,
        grid=(B, n_tiles, n_ff),
        in_specs=[
            pl.BlockSpec((1, tm, D_MODEL), tile_in),
            pl.BlockSpec((1, tm, POOL_WIDTH), tile_in),
            pl.BlockSpec((1, tm, n_mla), tile_in),
            pl.BlockSpec((1, tm, X_WIDTH), tile_in),
            _const_spec((D_MODEL, D_MODEL)),
            _const_spec((1, D_MODEL)),
            pl.BlockSpec((D_MODEL, TF), lambda b, i, c: (0, c)),
            pl.BlockSpec((D_MODEL, TF), lambda b, i, c: (0, c)),
            pl.BlockSpec((CONV_WIDTH, TF), lambda b, i, c: (0, c)),
            pl.BlockSpec((1, TF), lambda b, i, c: (0, c)),
            pl.BlockSpec((TF, D_MODEL), lambda b, i, c: (c, 0)),
        ],
        out_specs=pl.BlockSpec((1, tm, D_MODEL), lambda b, i, c: (b, i, 0)),
        out_shape=jax.ShapeDtypeStruct((B, S, D_MODEL), F32),
        scratch_shapes=[
            pltpu.VMEM((tm, D_MODEL), BF16),
            pltpu.VMEM((SUBLANES + tm, TF), F32),
            pltpu.VMEM((n_ff, SUBLANES, TF), F32),
        ],
        compiler_params=pltpu.CompilerParams(
            dimension_semantics=("arbitrary", "arbitrary", "arbitrary"),
            vmem_limit_bytes=VMEM_LIMIT),
        name="out_ffn",
    )(x, yp, ya, ym, w_o, g_ffn, w_gate, w_up, conv_w, conv_b, w_down)


def _dup_rope_cols(w):
    x1, x2 = w[..., :ROPE_HALF], w[..., ROPE_HALF:]
    return jnp.concatenate([x1, x2, x2, x1], axis=-1)


def _rope_tables(S):
    inv_freq = 1.0 / (ROPE_THETA ** (jnp.arange(ROPE_HALF, dtype=F32) / ROPE_HALF))
    ang = jnp.arange(S).astype(F32)[:, None] * inv_freq[None, :]
    cos, sin = jnp.cos(ang), jnp.sin(ang)
    zeros = jnp.zeros((S, 2 * ROPE_HALF), F32)
    return (jnp.concatenate([cos, cos, zeros], axis=-1),
            jnp.concatenate([-sin, sin, zeros], axis=-1), cos.T, sin.T)


def _layer(x, mem, g_mix, w_in, g_q_lat, w_q_up, g_kv_lat, w_kv_up, g_q_mla, g_k_mla,
           w_pool, pool_scale, g_mem, w_mem_kv, g_q_x, g_k_x, w_o, g_ffn,
           w_gate, w_up, conv_w, conv_b, w_down):
    S = x.shape[1]
    row = lambda v: v.reshape(1, -1)

    s0, s1, s2, s3 = _C_QLAT, _C_KVLAT, _C_KR, _C_KR + MLA_ROPE_DIM
    w_in_p = jnp.concatenate(
        [w_in[:, :s2], _dup_rope_cols(w_in[:, s2:s3]), w_in[:, s3:]], axis=1).astype(BF16)

    wq = w_q_up.reshape(Q_LORA_RANK, MLA_HEADS, MLA_QK_DIM)
    w_qT = jnp.pad(wq, ((0, 0), (0, 0), (0, HEAD_PAD - MLA_QK_DIM))).reshape(
        Q_LORA_RANK, MLA_HEADS * HEAD_PAD).T.astype(BF16)

    wkv = w_kv_up.reshape(KV_LORA_RANK, MLA_HEADS, MLA_NOPE_DIM + MLA_V_DIM)
    w_k = wkv[..., :MLA_NOPE_DIM].reshape(KV_LORA_RANK, -1).astype(BF16)
    w_vT = wkv[..., MLA_NOPE_DIM:].reshape(KV_LORA_RANK, -1).T.astype(BF16)

    cos_t, sin_t, cosT, sinT = _rope_tables(S)

    kT_mem, v_mem = _mem_kv(
        mem, row(g_mem), w_mem_kv[:, :X_WIDTH].T.astype(BF16), w_mem_kv[:, X_WIDTH:].astype(BF16),
        g_k_x.reshape(-1, 1))

    qT, k, vT, yp, ym = _in_proj(
        x, cos_t, sin_t, cosT, sinT, row(g_mix), w_in_p,
        row(g_q_lat), w_qT, g_q_mla.reshape(-1, 1),
        row(g_kv_lat), w_k, w_vT, row(g_k_mla[:MLA_NOPE_DIM]), row(_dup_rope_cols(g_k_mla[MLA_NOPE_DIM:])),
        w_pool.astype(BF16), row(pool_scale), row(g_q_x), kT_mem, v_mem)

    ya = _mla_attn(qT, k, vT)

    return _out_ffn(x, yp, ya, ym, w_o.astype(BF16), row(g_ffn),
                    w_gate.astype(BF16), w_up.astype(BF16), conv_w, row(conv_b), w_down.astype(BF16))


def kernel(x, mem, g_mix, w_in, g_q_lat, w_q_up, g_kv_lat, w_kv_up, g_q_mla, g_k_mla, w_pool, pool_scale, g_mem, w_mem_kv, g_q_x, g_k_x, w_o, g_ffn, w_gate, w_up, conv_w, conv_b, w_down):
    depth = g_mix.shape[0]
    for l in range(depth):
        x = _layer(x, mem, g_mix[l], w_in[l], g_q_lat[l], w_q_up[l], g_kv_lat[l], w_kv_up[l],
                   g_q_mla[l], g_k_mla[l], w_pool[l], pool_scale[l], g_mem[l], w_mem_kv[l],
                   g_q_x[l], g_k_x[l], w_o[l], g_ffn[l], w_gate[l], w_up[l], conv_w[l],
                   conv_b[l], w_down[l])
    return x
```

```python
import functools
import math

import jax
import jax.numpy as jnp
from jax import lax
from jax.experimental import pallas as pl
from jax.experimental.pallas import tpu as pltpu

D_MODEL = 2048
MEM_LEN = 256
POOL_WIDTH = 512
POOL_WINDOWS = (2, 4, 8, 16)
POOL_GROUP_DIM = 128
POOL_HALO = 16
MLA_HEADS = 8
MLA_V_DIM = 128
MLA_NOPE_DIM = 128
MLA_ROPE_DIM = 64
MLA_QK_DIM = MLA_NOPE_DIM + MLA_ROPE_DIM
Q_LORA_RANK = 512
KV_LORA_RANK = 256
X_HEADS = 4
X_WIDTH = 512
X_HEAD_DIM = 128
D_FF = 5632
CONV_WIDTH = 3
ROPE_THETA = 10000.0
NORM_EPS = 1e-6

LANES = 128
SUBLANES = 8
HEAD_PAD = 2 * LANES
VT_ROWS = MLA_V_DIM + 16
ROPE_HALF = MLA_ROPE_DIM // 2

_C_POOL = 0
_C_QLAT = _C_POOL + POOL_WIDTH
_C_KVLAT = _C_QLAT + Q_LORA_RANK
_C_KR = _C_KVLAT + KV_LORA_RANK
_C_MQ = _C_KR + LANES
IN_COLS_PAD = _C_MQ + X_WIDTH

TM_IN = 512
BQ = 1024
QC = 256
N_CHAIN = BQ // QC
N_BLK = 2
BK = BQ // N_BLK
ROW_CHUNK = 32
TM_FF = 512
TF = 512
VMEM_LIMIT = 56 * 1024 * 1024

BF16 = jnp.bfloat16
F32 = jnp.float32


def _dot(a, b):
    return jnp.dot(a, b, preferred_element_type=F32)


def _dot_nt(a, b):
    return lax.dot_general(a, b, (((1,), (1,)), ((), ())), preferred_element_type=F32)


def _rms_rows(xf, g):
    return xf * lax.rsqrt(jnp.mean(xf * xf, axis=-1, keepdims=True) + NORM_EPS) * g


def _rope_group(grp, cos_t, sin_t):
    return grp * cos_t + pltpu.roll(grp, 2 * ROPE_HALF, 1) * sin_t


def _const_spec(shape):
    nd = len(shape)
    return pl.BlockSpec(shape, lambda *_: (0,) * nd, pipeline_mode=pl.Buffered(1))


def _mem_kv_kernel(mem_ref, g_mem_ref, wkT_ref, wv_ref, gk_ref, kT_ref, v_ref):
    h = _rms_rows(mem_ref[0], g_mem_ref[...]).astype(BF16)
    kT = _dot_nt(wkT_ref[...], h)
    for hd in range(X_HEADS):
        rows = slice(hd * X_HEAD_DIM, (hd + 1) * X_HEAD_DIM)
        kh = kT[rows, :]
        r = lax.rsqrt(jnp.mean(kh * kh, axis=0, keepdims=True) + NORM_EPS)
        kT_ref[0, rows, :] = (kh * r * gk_ref[...]).astype(BF16)
    v_ref[0] = _dot(h, wv_ref[...]).astype(BF16)


def _mem_kv(mem, g_mem, wkT, wv, gk_col):
    B = mem.shape[0]
    return pl.pallas_call(
        _mem_kv_kernel,
        grid=(B,),
        in_specs=[
            pl.BlockSpec((1, MEM_LEN, D_MODEL), lambda b: (b, 0, 0)),
            _const_spec((1, D_MODEL)),
            _const_spec((X_WIDTH, D_MODEL)),
            _const_spec((D_MODEL, X_WIDTH)),
            _const_spec((X_HEAD_DIM, 1)),
        ],
        out_specs=[
            pl.BlockSpec((1, X_WIDTH, MEM_LEN), lambda b: (b, 0, 0)),
            pl.BlockSpec((1, MEM_LEN, X_WIDTH), lambda b: (b, 0, 0)),
        ],
        out_shape=[
            jax.ShapeDtypeStruct((B, X_WIDTH, MEM_LEN), BF16),
            jax.ShapeDtypeStruct((B, MEM_LEN, X_WIDTH), BF16),
        ],
        compiler_params=pltpu.CompilerParams(
            dimension_semantics=("arbitrary",), vmem_limit_bytes=VMEM_LIMIT),
        name="mem_kv",
    )(mem, g_mem, wkT, wv, gk_col)


def _in_proj_kernel(x_ref, cos_ref, sin_ref, cosT_ref, sinT_ref, g_mix_ref, w_in_ref,
                    g_ql_ref, w_qT_ref, gq_col_ref,
                    g_kvl_ref, w_k_ref, w_vT_ref, gk_nope_ref, gk_grp_ref,
                    w_pool_ref, pool_scale_ref,
                    g_qx_ref, kT_mem_ref, v_mem_ref,
                    qT_ref, k_ref, vT_ref, yp_ref, ym_ref,
                    zp_ref):
    i = pl.program_id(1)
    tm = x_ref.shape[1]
    cos_t = cos_ref[...]
    sin_t = sin_ref[...]

    @pl.when(i == 0)
    def _():
        zp_ref[0:POOL_HALO, :] = jnp.zeros((POOL_HALO, POOL_WIDTH), F32)

    @pl.when(i > 0)
    def _():
        zp_ref[0:POOL_HALO, :] = zp_ref[tm:tm + POOL_HALO, :]

    h = _rms_rows(x_ref[0], g_mix_ref[...]).astype(BF16)
    z = _dot(h, w_in_ref[...])
    z_pool = z[:, _C_POOL:_C_POOL + POOL_WIDTH]
    z_q = z[:, _C_QLAT:_C_QLAT + Q_LORA_RANK]
    z_kv = z[:, _C_KVLAT:_C_KVLAT + KV_LORA_RANK]
    kr = z[:, _C_KR:_C_KR + LANES]
    z_mq = z[:, _C_MQ:_C_MQ + X_WIDTH]

    zp_ref[POOL_HALO:POOL_HALO + tm, :] = z_pool
    t = i * tm + lax.broadcasted_iota(jnp.int32, (tm, 1), 0)
    for g, w in enumerate(POOL_WINDOWS):
        cols = slice(g * POOL_GROUP_DIM, (g + 1) * POOL_GROUP_DIM)
        s = z_pool[:, cols]
        for j in range(1, w):
            s = s + zp_ref[POOL_HALO - j:POOL_HALO - j + tm, cols]
        cnt = jnp.minimum(t + 1, w).astype(F32)
        d = s / cnt - z_pool[:, cols]
        y = _dot(d.astype(BF16), w_pool_ref[g]) * pool_scale_ref[:, cols]
        yp_ref[0, :, cols] = y.astype(BF16)

    q_scale = math.log2(math.e) / math.sqrt(MLA_QK_DIM)
    hq = _rms_rows(z_q, g_ql_ref[...]).astype(BF16)
    qfT = _dot_nt(w_qT_ref[...], hq)
    cosT = cosT_ref[...]
    sinT = sinT_ref[...]
    for hd in range(MLA_HEADS):
        r0 = hd * HEAD_PAD
        raw = qfT[r0:r0 + MLA_QK_DIM, :]
        ssq = jnp.sum(raw * raw, axis=0, keepdims=True)
        r = lax.rsqrt(ssq * (1.0 / MLA_QK_DIM) + NORM_EPS) * q_scale
        g = raw * gq_col_ref[...]
        x1 = g[MLA_NOPE_DIM:MLA_NOPE_DIM + ROPE_HALF, :]
        x2 = g[MLA_NOPE_DIM + ROPE_HALF:MLA_QK_DIM, :]
        qT_ref[0, r0:r0 + MLA_NOPE_DIM, :] = (g[0:MLA_NOPE_DIM, :] * r).astype(BF16)
        qT_ref[0, r0 + MLA_NOPE_DIM:r0 + MLA_NOPE_DIM + ROPE_HALF, :] = (
            (x1 * cosT - x2 * sinT) * r).astype(BF16)
        qT_ref[0, r0 + MLA_NOPE_DIM + ROPE_HALF:r0 + MLA_QK_DIM, :] = (
            (x2 * cosT + x1 * sinT) * r).astype(BF16)
        qT_ref[0, r0 + MLA_QK_DIM:r0 + HEAD_PAD, :] = jnp.zeros((HEAD_PAD - MLA_QK_DIM, tm), BF16)

    hkv = _rms_rows(z_kv, g_kvl_ref[...]).astype(BF16)
    kn = _dot(hkv, w_k_ref[...])
    vT = _dot_nt(w_vT_ref[...], hkv).astype(BF16)
    for hd in range(MLA_HEADS):
        r0 = hd * VT_ROWS
        vT_ref[0, r0:r0 + MLA_V_DIM, :] = vT[hd * MLA_V_DIM:(hd + 1) * MLA_V_DIM, :]
        vT_ref[0, r0 + MLA_V_DIM:r0 + VT_ROWS, :] = jnp.ones((VT_ROWS - MLA_V_DIM, tm), BF16)
    kr_ssq = 0.5 * jnp.sum(kr * kr, axis=-1, keepdims=True)
    kr_rot = _rope_group(kr * gk_grp_ref[...], cos_t, sin_t)
    for hd in range(MLA_HEADS):
        nope = kn[:, hd * LANES:(hd + 1) * LANES]
        ssq = jnp.sum(nope * nope, axis=-1, keepdims=True) + kr_ssq
        r = lax.rsqrt(ssq * (1.0 / MLA_QK_DIM) + NORM_EPS)
        c0 = hd * HEAD_PAD
        k_ref[0, :, c0:c0 + LANES] = (nope * r * gk_nope_ref[...]).astype(BF16)
        k_ref[0, :, c0 + LANES:c0 + HEAD_PAD] = (kr_rot * r).astype(BF16)

    x_scale = 1.0 / math.sqrt(X_HEAD_DIM)
    for hd in range(X_HEADS):
        cols = slice(hd * X_HEAD_DIM, (hd + 1) * X_HEAD_DIM)
        qn = (_rms_rows(z_mq[:, cols], g_qx_ref[...]) * x_scale).astype(BF16)
        s = _dot(qn, kT_mem_ref[0, cols, :])
        m = jnp.max(s, axis=-1, keepdims=True)
        p = jnp.exp(s - m)
        l = jnp.sum(p, axis=-1, keepdims=True)
        o = _dot(p.astype(BF16), v_mem_ref[0, :, cols]) / l
        ym_ref[0, :, cols] = o.astype(BF16)


def _in_proj(x, cos_t, sin_t, cosT, sinT, g_mix, w_in_p, g_ql, w_qT, gq_col,
             g_kvl, w_k, w_vT, gk_nope, gk_grp, w_pool, pool_scale, g_qx, kT_mem, v_mem):
    B, S, _ = x.shape
    tm = TM_IN
    n_kv = MLA_HEADS * MLA_V_DIM
    in_specs = [
        pl.BlockSpec((1, tm, D_MODEL), lambda b, i: (b, i, 0)),
        pl.BlockSpec((tm, LANES), lambda b, i: (i, 0)),
        pl.BlockSpec((tm, LANES), lambda b, i: (i, 0)),
        pl.BlockSpec((ROPE_HALF, tm), lambda b, i: (0, i)),
        pl.BlockSpec((ROPE_HALF, tm), lambda b, i: (0, i)),
        _const_spec((1, D_MODEL)),
        _const_spec((D_MODEL, IN_COLS_PAD)),
        _const_spec((1, Q_LORA_RANK)),
        _const_spec((MLA_HEADS * HEAD_PAD, Q_LORA_RANK)),
        _const_spec((MLA_QK_DIM, 1)),
        _const_spec((1, KV_LORA_RANK)),
        _const_spec((KV_LORA_RANK, n_kv)),
        _const_spec((n_kv, KV_LORA_RANK)),
        _const_spec((1, LANES)),
        _const_spec((1, LANES)),
        _const_spec((len(POOL_WINDOWS), POOL_GROUP_DIM, POOL_GROUP_DIM)),
        _const_spec((1, POOL_WIDTH)),
        _const_spec((1, X_HEAD_DIM)),
        pl.BlockSpec((1, X_WIDTH, MEM_LEN), lambda b, i: (b, 0, 0)),
        pl.BlockSpec((1, MEM_LEN, X_WIDTH), lambda b, i: (b, 0, 0)),
    ]
    out_specs = [
        pl.BlockSpec((1, MLA_HEADS * HEAD_PAD, tm), lambda b, i: (b, 0, i)),
        pl.BlockSpec((1, tm, MLA_HEADS * HEAD_PAD), lambda b, i: (b, i, 0)),
        pl.BlockSpec((1, MLA_HEADS * VT_ROWS, tm), lambda b, i: (b, 0, i)),
        pl.BlockSpec((1, tm, POOL_WIDTH), lambda b, i: (b, i, 0)),
        pl.BlockSpec((1, tm, X_WIDTH), lambda b, i: (b, i, 0)),
    ]
    out_shape = [
        jax.ShapeDtypeStruct((B, MLA_HEADS * HEAD_PAD, S), BF16),
        jax.ShapeDtypeStruct((B, S, MLA_HEADS * HEAD_PAD), BF16),
        jax.ShapeDtypeStruct((B, MLA_HEADS * VT_ROWS, S), BF16),
        jax.ShapeDtypeStruct((B, S, POOL_WIDTH), BF16),
        jax.ShapeDtypeStruct((B, S, X_WIDTH), BF16),
    ]
    return pl.pallas_call(
        _in_proj_kernel,
        grid=(B, S // tm),
        in_specs=in_specs,
        out_specs=out_specs,
        out_shape=out_shape,
        scratch_shapes=[pltpu.VMEM((POOL_HALO + tm, POOL_WIDTH), F32)],
        compiler_params=pltpu.CompilerParams(
            dimension_semantics=("arbitrary", "arbitrary"), vmem_limit_bytes=VMEM_LIMIT),
        name="in_proj",
    )(x, cos_t, sin_t, cosT, sinT, g_mix, w_in_p, g_ql, w_qT, gq_col,
      g_kvl, w_k, w_vT, gk_nope, gk_grp, w_pool, pool_scale, g_qx, kT_mem, v_mem)


def _mla_attn_kernel(qT_ref, k_ref, vT_ref, o_ref, s_ref, mb_ref, p_ref, m_ref, acc_ref):
    n_q = k_ref.shape[1] // BQ
    chains = tuple(range(N_CHAIN))

    def scores(j, start, q0, cs):
        k = k_ref[0, pl.ds(start, BK), :]
        for c in cs:
            qT = qT_ref[0, :, pl.ds(pl.multiple_of(q0 + c * QC, QC), QC)]
            sT = _dot(k, qT)
            s_ref[j, c] = sT
            mb_ref[j, c] = jnp.max(sT, axis=0, keepdims=True)

    def mask(j, c, off):
        rr = lax.broadcasted_iota(jnp.int32, (BK, QC), 0)
        qq = lax.broadcasted_iota(jnp.int32, (BK, QC), 1)
        sT = jnp.where(rr <= qq + off, s_ref[j, c], -jnp.inf)
        s_ref[j, c] = sT
        mb_ref[j, c] = jnp.max(sT, axis=0, keepdims=True)

    def update(j, start, cs):
        for c in cs:
            m_old = m_ref[c]
            m_new = jnp.maximum(m_old, mb_ref[j, c])
            alpha = jnp.exp2(m_old - m_new)
            for r in range(0, BK, ROW_CHUNK):
                x = s_ref[j, c, r:r + ROW_CHUNK, :] - m_new
                p_ref[c, r:r + ROW_CHUNK, :] = jnp.exp2(x).astype(BF16)
            pv = _dot(vT_ref[0, :, pl.ds(start, BK)], p_ref[c])
            acc_ref[c] = acc_ref[c] * alpha + pv
            m_ref[c] = m_new

    in_blk = lambda d: tuple(c for c in chains if (c * QC) // BK == d)
    from_blk = lambda d: tuple(c for c in chains if (c * QC) // BK >= d)

    scores(0, 0, 0, chains)

    def qblock(i, carry):
        q0 = pl.multiple_of(i * BQ, BQ)
        m_ref[...] = jnp.full(m_ref.shape, -jnp.inf, F32)
        acc_ref[...] = jnp.zeros(acc_ref.shape, F32)

        def trip(t, carry):
            b0 = pl.multiple_of(t * (2 * BK), 2 * BK)
            scores(1, b0 + BK, q0, chains)
            update(0, b0, chains)

            @pl.when(t >= 0)
            def _():
                scores(0, b0 + 2 * BK, q0, chains)
                update(1, b0 + BK, chains)

            return carry

        lax.fori_loop(0, i, trip, 0)

        scores(1, q0 + BK, q0, from_blk(1))
        for c in in_blk(0):
            mask(0, c, c * QC)
        update(0, q0, chains)

        @pl.when(i >= 0)
        def _():
            q_next = pl.multiple_of(jnp.minimum(i + 1, n_q - 1) * BQ, BQ)
            scores(0, 0, q_next, chains)
            for c in in_blk(1):
                mask(1, c, c * QC - BK)
            update(1, q0 + BK, from_blk(1))
            for c in chains:
                o = acc_ref[c, 0:MLA_V_DIM, :] / acc_ref[c, MLA_V_DIM:MLA_V_DIM + 1, :]
                o_ref[0, pl.ds(pl.multiple_of(q0 + c * QC, QC), QC), :] = o.T.astype(BF16)

        return carry

    lax.fori_loop(0, n_q, qblock, 0)


def _mla_attn(qT, k, vT):
    B, S, _ = k.shape
    assert N_BLK == 2 and BQ % QC == 0 and BK % QC == 0 and BK % ROW_CHUNK == 0 and S % BQ == 0
    return pl.pallas_call(
        _mla_attn_kernel,
        grid=(B, MLA_HEADS),
        in_specs=[
            pl.BlockSpec((1, HEAD_PAD, S), lambda b, h: (b, h, 0)),
            pl.BlockSpec((1, S, HEAD_PAD), lambda b, h: (b, 0, h)),
            pl.BlockSpec((1, VT_ROWS, S), lambda b, h: (b, h, 0)),
        ],
        out_specs=pl.BlockSpec((1, S, MLA_V_DIM), lambda b, h: (b, 0, h)),
        out_shape=jax.ShapeDtypeStruct((B, S, MLA_HEADS * MLA_V_DIM), BF16),
        scratch_shapes=[
            pltpu.VMEM((2, N_CHAIN, BK, QC), F32),
            pltpu.VMEM((2, N_CHAIN, 1, QC), F32),
            pltpu.VMEM((N_CHAIN, BK, QC), BF16),
            pltpu.VMEM((N_CHAIN, 1, QC), F32),
            pltpu.VMEM((N_CHAIN, VT_ROWS, QC), F32),
        ],
        compiler_params=pltpu.CompilerParams(
            dimension_semantics=("arbitrary", "arbitrary"),
            vmem_limit_bytes=VMEM_LIMIT),
        name="mla_attn",
    )(qT, k, vT)


def _out_ffn_kernel(x_ref, yp_ref, ya_ref, ym_ref, w_o_ref, g_ffn_ref,
                    w_gate_ref, w_up_ref, conv_w_ref, conv_b_ref, w_down_ref,
                    o_ref, h2_ref, gbuf_ref, carry_ref):
    i = pl.program_id(1)
    c = pl.program_id(2)
    tm = x_ref.shape[1]
    n_mla = MLA_HEADS * MLA_V_DIM

    @pl.when(c == 0)
    def _():
        x1 = (x_ref[0]
              + _dot(yp_ref[0], w_o_ref[0:POOL_WIDTH, :])
              + _dot(ya_ref[0], w_o_ref[POOL_WIDTH:POOL_WIDTH + n_mla, :])
              + _dot(ym_ref[0], w_o_ref[POOL_WIDTH + n_mla:, :]))
        o_ref[0] = x1
        h2_ref[...] = _rms_rows(x1, g_ffn_ref[...]).astype(BF16)

        @pl.when(i == 0)
        def _():
            carry_ref[...] = jnp.zeros(carry_ref.shape, F32)

    h2 = h2_ref[...]
    g = _dot(h2, w_gate_ref[...])
    u = _dot(h2, w_up_ref[...])

    gbuf_ref[0:SUBLANES, :] = carry_ref[c]
    gbuf_ref[SUBLANES:SUBLANES + tm, :] = g
    carry_ref[c] = g[tm - SUBLANES:tm, :]
    gc = conv_b_ref[...] + conv_w_ref[CONV_WIDTH - 1:CONV_WIDTH, :] * g
    for j in range(CONV_WIDTH - 1):
        lag = CONV_WIDTH - 1 - j
        gc = gc + conv_w_ref[j:j + 1, :] * gbuf_ref[SUBLANES - lag:SUBLANES - lag + tm, :]
    act = (gc / (1.0 + jnp.exp(-gc))) * u
    o_ref[0] += _dot(act.astype(BF16), w_down_ref[...])


def _out_ffn(x, yp, ya, ym, w_o, g_ffn, w_gate, w_up, conv_w, conv_b, w_down):
    B, S, _ = x.shape
    tm = TM_FF
    n_mla = MLA_HEADS * MLA_V_DIM
    n_ff = D_FF // TF
    assert n_ff * TF == D_FF
    n_tiles = S // tm

    def tile_in(b, i, c):
        return (b, jnp.minimum(i + jnp.minimum(c, 1), n_tiles - 1), 0)

    return pl.pallas_call(
        _out_ffn_kernel,
        grid=(B, n_tiles, n_ff),
        in_specs=[
            pl.BlockSpec((1, tm, D_MODEL), tile_in),
            pl.BlockSpec((1, tm, POOL_WIDTH), tile_in),
            pl.BlockSpec((1, tm, n_mla), tile_in),
            pl.BlockSpec((1, tm, X_WIDTH), tile_in),
            _const_spec((D_MODEL, D_MODEL)),
            _const_spec((1, D_MODEL)),
            pl.BlockSpec((D_MODEL, TF), lambda b, i, c: (0, c)),
            pl.BlockSpec((D_MODEL, TF), lambda b, i, c: (0, c)),
            pl.BlockSpec((CONV_WIDTH, TF), lambda b, i, c: (0, c)),
            pl.BlockSpec((1, TF), lambda b, i, c: (0, c)),
            pl.BlockSpec((TF, D_MODEL), lambda b, i, c: (c, 0)),
        ],
        out_specs=pl.BlockSpec((1, tm, D_MODEL), lambda b, i, c: (b, i, 0)),
        out_shape=jax.ShapeDtypeStruct((B, S, D_MODEL), F32),
        scratch_shapes=[
            pltpu.VMEM((tm, D_MODEL), BF16),
            pltpu.VMEM((SUBLANES + tm, TF), F32),
            pltpu.VMEM((n_ff, SUBLANES, TF), F32),
        ],
        compiler_params=pltpu.CompilerParams(
            dimension_semantics=("arbitrary", "arbitrary", "arbitrary"),
            vmem_limit_bytes=VMEM_LIMIT),
        name="out_ffn",
    )(x, yp, ya, ym, w_o, g_ffn, w_gate, w_up, conv_w, conv_b, w_down)


def _dup_rope_cols(w):
    x1, x2 = w[..., :ROPE_HALF], w[..., ROPE_HALF:]
    return jnp.concatenate([x1, x2, x2, x1], axis=-1)


def _rope_tables(S):
    inv_freq = 1.0 / (ROPE_THETA ** (jnp.arange(ROPE_HALF, dtype=F32) / ROPE_HALF))
    ang = jnp.arange(S).astype(F32)[:, None] * inv_freq[None, :]
    cos, sin = jnp.cos(ang), jnp.sin(ang)
    zeros = jnp.zeros((S, 2 * ROPE_HALF), F32)
    return (jnp.concatenate([cos, cos, zeros], axis=-1),
            jnp.concatenate([-sin, sin, zeros], axis=-1), cos.T, sin.T)


def _layer(x, mem, g_mix, w_in, g_q_lat, w_q_up, g_kv_lat, w_kv_up, g_q_mla, g_k_mla,
           w_pool, pool_scale, g_mem, w_mem_kv, g_q_x, g_k_x, w_o, g_ffn,
           w_gate, w_up, conv_w, conv_b, w_down):
    S = x.shape[1]
    row = lambda v: v.reshape(1, -1)

    s0, s1, s2, s3 = _C_QLAT, _C_KVLAT, _C_KR, _C_KR + MLA_ROPE_DIM
    w_in_p = jnp.concatenate(
        [w_in[:, :s2], _dup_rope_cols(w_in[:, s2:s3]), w_in[:, s3:]], axis=1).astype(BF16)

    wq = w_q_up.reshape(Q_LORA_RANK, MLA_HEADS, MLA_QK_DIM)
    w_qT = jnp.pad(wq, ((0, 0), (0, 0), (0, HEAD_PAD - MLA_QK_DIM))).reshape(
        Q_LORA_RANK, MLA_HEADS * HEAD_PAD).T.astype(BF16)

    wkv = w_kv_up.reshape(KV_LORA_RANK, MLA_HEADS, MLA_NOPE_DIM + MLA_V_DIM)
    w_k = wkv[..., :MLA_NOPE_DIM].reshape(KV_LORA_RANK, -1).astype(BF16)
    w_vT = wkv[..., MLA_NOPE_DIM:].reshape(KV_LORA_RANK, -1).T.astype(BF16)

    cos_t, sin_t, cosT, sinT = _rope_tables(S)

    kT_mem, v_mem = _mem_kv(
        mem, row(g_mem), w_mem_kv[:, :X_WIDTH].T.astype(BF16), w_mem_kv[:, X_WIDTH:].astype(BF16),
        g_k_x.reshape(-1, 1))

    qT, k, vT, yp, ym = _in_proj(
        x, cos_t, sin_t, cosT, sinT, row(g_mix), w_in_p,
        row(g_q_lat), w_qT, g_q_mla.reshape(-1, 1),
        row(g_kv_lat), w_k, w_vT, row(g_k_mla[:MLA_NOPE_DIM]), row(_dup_rope_cols(g_k_mla[MLA_NOPE_DIM:])),
        w_pool.astype(BF16), row(pool_scale), row(g_q_x), kT_mem, v_mem)

    ya = _mla_attn(qT, k, vT)

    return _out_ffn(x, yp, ya, ym, w_o.astype(BF16), row(g_ffn),
                    w_gate.astype(BF16), w_up.astype(BF16), conv_w, row(conv_b), w_down.astype(BF16))


def kernel(x, mem, g_mix, w_in, g_q_lat, w_q_up, g_kv_lat, w_kv_up, g_q_mla, g_k_mla, w_pool, pool_scale, g_mem, w_mem_kv, g_q_x, g_k_x, w_o, g_ffn, w_gate, w_up, conv_w, conv_b, w_down):
    depth = g_mix.shape[0]
    for l in range(depth):
        x = _layer(x, mem, g_mix[l], w_in[l], g_q_lat[l], w_q_up[l], g_kv_lat[l], w_kv_up[l],
                   g_q_mla[l], g_k_mla[l], w_pool[l], pool_scale[l], g_mem[l], w_mem_kv[l],
                   g_q_x[l], g_k_x[l], w_o[l], g_ffn[l], w_gate[l], w_up[l], conv_w[l],
                   conv_b[l], w_down[l])
    return x
```

```python
import functools
import math

import jax
import jax.numpy as jnp
from jax import lax
from jax.experimental import pallas as pl
from jax.experimental.pallas import tpu as pltpu

D_MODEL = 2048
MEM_LEN = 256
POOL_WIDTH = 512
POOL_WINDOWS = (2, 4, 8, 16)
POOL_GROUP_DIM = 128
POOL_HALO = 16
MLA_HEADS = 8
MLA_V_DIM = 128
MLA_NOPE_DIM = 128
MLA_ROPE_DIM = 64
MLA_QK_DIM = MLA_NOPE_DIM + MLA_ROPE_DIM
Q_LORA_RANK = 512
KV_LORA_RANK = 256
X_HEADS = 4
X_WIDTH = 512
X_HEAD_DIM = 128
D_FF = 5632
CONV_WIDTH = 3
ROPE_THETA = 10000.0
NORM_EPS = 1e-6

LANES = 128
SUBLANES = 8
HEAD_PAD = 2 * LANES
VT_ROWS = MLA_V_DIM + 16
ROPE_HALF = MLA_ROPE_DIM // 2

_C_POOL = 0
_C_QLAT = _C_POOL + POOL_WIDTH
_C_KVLAT = _C_QLAT + Q_LORA_RANK
_C_KR = _C_KVLAT + KV_LORA_RANK
_C_MQ = _C_KR + LANES
IN_COLS_PAD = _C_MQ + X_WIDTH

TM_IN = 512
BQ = 2048
QC = 256
N_CHAIN = BQ // QC
N_BLK = 2
BK = BQ // N_BLK
ROW_CHUNK = 32
TM_FF = 512
TF = 512
VMEM_LIMIT = 56 * 1024 * 1024

BF16 = jnp.bfloat16
F32 = jnp.float32


def _dot(a, b):
    return jnp.dot(a, b, preferred_element_type=F32)


def _dot_nt(a, b):
    return lax.dot_general(a, b, (((1,), (1,)), ((), ())), preferred_element_type=F32)


def _rms_rows(xf, g):
    return xf * lax.rsqrt(jnp.mean(xf * xf, axis=-1, keepdims=True) + NORM_EPS) * g


def _rope_group(grp, cos_t, sin_t):
    return grp * cos_t + pltpu.roll(grp, 2 * ROPE_HALF, 1) * sin_t


def _const_spec(shape):
    nd = len(shape)
    return pl.BlockSpec(shape, lambda *_: (0,) * nd, pipeline_mode=pl.Buffered(1))


def _mem_kv_kernel(mem_ref, g_mem_ref, wkT_ref, wv_ref, gk_ref, kT_ref, v_ref):
    h = _rms_rows(mem_ref[0], g_mem_ref[...]).astype(BF16)
    kT = _dot_nt(wkT_ref[...], h)
    for hd in range(X_HEADS):
        rows = slice(hd * X_HEAD_DIM, (hd + 1) * X_HEAD_DIM)
        kh = kT[rows, :]
        r = lax.rsqrt(jnp.mean(kh * kh, axis=0, keepdims=True) + NORM_EPS)
        kT_ref[0, rows, :] = (kh * r * gk_ref[...]).astype(BF16)
    v_ref[0] = _dot(h, wv_ref[...]).astype(BF16)


def _mem_kv(mem, g_mem, wkT, wv, gk_col):
    B = mem.shape[0]
    return pl.pallas_call(
        _mem_kv_kernel,
        grid=(B,),
        in_specs=[
            pl.BlockSpec((1, MEM_LEN, D_MODEL), lambda b: (b, 0, 0)),
            _const_spec((1, D_MODEL)),
            _const_spec((X_WIDTH, D_MODEL)),
            _const_spec((D_MODEL, X_WIDTH)),
            _const_spec((X_HEAD_DIM, 1)),
        ],
        out_specs=[
            pl.BlockSpec((1, X_WIDTH, MEM_LEN), lambda b: (b, 0, 0)),
            pl.BlockSpec((1, MEM_LEN, X_WIDTH), lambda b: (b, 0, 0)),
        ],
        out_shape=[
            jax.ShapeDtypeStruct((B, X_WIDTH, MEM_LEN), BF16),
            jax.ShapeDtypeStruct((B, MEM_LEN, X_WIDTH), BF16),
        ],
        compiler_params=pltpu.CompilerParams(
            dimension_semantics=("arbitrary",), vmem_limit_bytes=VMEM_LIMIT),
        name="mem_kv",
    )(mem, g_mem, wkT, wv, gk_col)


def _in_proj_kernel(x_ref, cos_ref, sin_ref, cosT_ref, sinT_ref, g_mix_ref, w_in_ref,
                    g_ql_ref, w_qT_ref, gq_col_ref,
                    g_kvl_ref, w_k_ref, w_vT_ref, gk_nope_ref, gk_grp_ref,
                    w_pool_ref, pool_scale_ref,
                    g_qx_ref, kT_mem_ref, v_mem_ref,
                    qT_ref, k_ref, vT_ref, yp_ref, ym_ref,
                    zp_ref):
    i = pl.program_id(1)
    tm = x_ref.shape[1]
    cos_t = cos_ref[...]
    sin_t = sin_ref[...]

    @pl.when(i == 0)
    def _():
        zp_ref[0:POOL_HALO, :] = jnp.zeros((POOL_HALO, POOL_WIDTH), F32)

    @pl.when(i > 0)
    def _():
        zp_ref[0:POOL_HALO, :] = zp_ref[tm:tm + POOL_HALO, :]

    h = _rms_rows(x_ref[0], g_mix_ref[...]).astype(BF16)
    z = _dot(h, w_in_ref[...])
    z_pool = z[:, _C_POOL:_C_POOL + POOL_WIDTH]
    z_q = z[:, _C_QLAT:_C_QLAT + Q_LORA_RANK]
    z_kv = z[:, _C_KVLAT:_C_KVLAT + KV_LORA_RANK]
    kr = z[:, _C_KR:_C_KR + LANES]
    z_mq = z[:, _C_MQ:_C_MQ + X_WIDTH]

    zp_ref[POOL_HALO:POOL_HALO + tm, :] = z_pool
    t = i * tm + lax.broadcasted_iota(jnp.int32, (tm, 1), 0)
    for g, w in enumerate(POOL_WINDOWS):
        cols = slice(g * POOL_GROUP_DIM, (g + 1) * POOL_GROUP_DIM)
        s = z_pool[:, cols]
        for j in range(1, w):
            s = s + zp_ref[POOL_HALO - j:POOL_HALO - j + tm, cols]
        cnt = jnp.minimum(t + 1, w).astype(F32)
        d = s / cnt - z_pool[:, cols]
        y = _dot(d.astype(BF16), w_pool_ref[g]) * pool_scale_ref[:, cols]
        yp_ref[0, :, cols] = y.astype(BF16)

    q_scale = math.log2(math.e) / math.sqrt(MLA_QK_DIM)
    hq = _rms_rows(z_q, g_ql_ref[...]).astype(BF16)
    qfT = _dot_nt(w_qT_ref[...], hq)
    cosT = cosT_ref[...]
    sinT = sinT_ref[...]
    for hd in range(MLA_HEADS):
        r0 = hd * HEAD_PAD
        raw = qfT[r0:r0 + MLA_QK_DIM, :]
        ssq = jnp.sum(raw * raw, axis=0, keepdims=True)
        r = lax.rsqrt(ssq * (1.0 / MLA_QK_DIM) + NORM_EPS) * q_scale
        g = raw * gq_col_ref[...]
        x1 = g[MLA_NOPE_DIM:MLA_NOPE_DIM + ROPE_HALF, :]
        x2 = g[MLA_NOPE_DIM + ROPE_HALF:MLA_QK_DIM, :]
        qT_ref[0, r0:r0 + MLA_NOPE_DIM, :] = (g[0:MLA_NOPE_DIM, :] * r).astype(BF16)
        qT_ref[0, r0 + MLA_NOPE_DIM:r0 + MLA_NOPE_DIM + ROPE_HALF, :] = (
            (x1 * cosT - x2 * sinT) * r).astype(BF16)
        qT_ref[0, r0 + MLA_NOPE_DIM + ROPE_HALF:r0 + MLA_QK_DIM, :] = (
            (x2 * cosT + x1 * sinT) * r).astype(BF16)
        qT_ref[0, r0 + MLA_QK_DIM:r0 + HEAD_PAD, :] = jnp.zeros((HEAD_PAD - MLA_QK_DIM, tm), BF16)

    hkv = _rms_rows(z_kv, g_kvl_ref[...]).astype(BF16)
    kn = _dot(hkv, w_k_ref[...])
    vT = _dot_nt(w_vT_ref[...], hkv).astype(BF16)
    for hd in range(MLA_HEADS):
        r0 = hd * VT_ROWS
        vT_ref[0, r0:r0 + MLA_V_DIM, :] = vT[hd * MLA_V_DIM:(hd + 1) * MLA_V_DIM, :]
        vT_ref[0, r0 + MLA_V_DIM:r0 + VT_ROWS, :] = jnp.ones((VT_ROWS - MLA_V_DIM, tm), BF16)
    kr_ssq = 0.5 * jnp.sum(kr * kr, axis=-1, keepdims=True)
    kr_rot = _rope_group(kr * gk_grp_ref[...], cos_t, sin_t)
    for hd in range(MLA_HEADS):
        nope = kn[:, hd * LANES:(hd + 1) * LANES]
        ssq = jnp.sum(nope * nope, axis=-1, keepdims=True) + kr_ssq
        r = lax.rsqrt(ssq * (1.0 / MLA_QK_DIM) + NORM_EPS)
        c0 = hd * HEAD_PAD
        k_ref[0, :, c0:c0 + LANES] = (nope * r * gk_nope_ref[...]).astype(BF16)
        k_ref[0, :, c0 + LANES:c0 + HEAD_PAD] = (kr_rot * r).astype(BF16)

    x_scale = 1.0 / math.sqrt(X_HEAD_DIM)
    for hd in range(X_HEADS):
        cols = slice(hd * X_HEAD_DIM, (hd + 1) * X_HEAD_DIM)
        qn = (_rms_rows(z_mq[:, cols], g_qx_ref[...]) * x_scale).astype(BF16)
        s = _dot(qn, kT_mem_ref[0, cols, :])
        m = jnp.max(s, axis=-1, keepdims=True)
        p = jnp.exp(s - m)
        l = jnp.sum(p, axis=-1, keepdims=True)
        o = _dot(p.astype(BF16), v_mem_ref[0, :, cols]) / l
        ym_ref[0, :, cols] = o.astype(BF16)


def _in_proj(x, cos_t, sin_t, cosT, sinT, g_mix, w_in_p, g_ql, w_qT, gq_col,
             g_kvl, w_k, w_vT, gk_nope, gk_grp, w_pool, pool_scale, g_qx, kT_mem, v_mem):
    B, S, _ = x.shape
    tm = TM_IN
    n_kv = MLA_HEADS * MLA_V_DIM
    in_specs = [
        pl.BlockSpec((1, tm, D_MODEL), lambda b, i: (b, i, 0)),
        pl.BlockSpec((tm, LANES), lambda b, i: (i, 0)),
        pl.BlockSpec((tm, LANES), lambda b, i: (i, 0)),
        pl.BlockSpec((ROPE_HALF, tm), lambda b, i: (0, i)),
        pl.BlockSpec((ROPE_HALF, tm), lambda b, i: (0, i)),
        _const_spec((1, D_MODEL)),
        _const_spec((D_MODEL, IN_COLS_PAD)),
        _const_spec((1, Q_LORA_RANK)),
        _const_spec((MLA_HEADS * HEAD_PAD, Q_LORA_RANK)),
        _const_spec((MLA_QK_DIM, 1)),
        _const_spec((1, KV_LORA_RANK)),
        _const_spec((KV_LORA_RANK, n_kv)),
        _const_spec((n_kv, KV_LORA_RANK)),
        _const_spec((1, LANES)),
        _const_spec((1, LANES)),
        _const_spec((len(POOL_WINDOWS), POOL_GROUP_DIM, POOL_GROUP_DIM)),
        _const_spec((1, POOL_WIDTH)),
        _const_spec((1, X_HEAD_DIM)),
        pl.BlockSpec((1, X_WIDTH, MEM_LEN), lambda b, i: (b, 0, 0)),
        pl.BlockSpec((1, MEM_LEN, X_WIDTH), lambda b, i: (b, 0, 0)),
    ]
    out_specs = [
        pl.BlockSpec((1, MLA_HEADS * HEAD_PAD, tm), lambda b, i: (b, 0, i)),
        pl.BlockSpec((1, tm, MLA_HEADS * HEAD_PAD), lambda b, i: (b, i, 0)),
        pl.BlockSpec((1, MLA_HEADS * VT_ROWS, tm), lambda b, i: (b, 0, i)),
        pl.BlockSpec((1, tm, POOL_WIDTH), lambda b, i: (b, i, 0)),
        pl.BlockSpec((1, tm, X_WIDTH), lambda b, i: (b, i, 0)),
    ]
    out_shape = [
        jax.ShapeDtypeStruct((B, MLA_HEADS * HEAD_PAD, S), BF16),
        jax.ShapeDtypeStruct((B, S, MLA_HEADS * HEAD_PAD), BF16),
        jax.ShapeDtypeStruct((B, MLA_HEADS * VT_ROWS, S), BF16),
        jax.ShapeDtypeStruct((B, S, POOL_WIDTH), BF16),
        jax.ShapeDtypeStruct((B, S, X_WIDTH), BF16),
    ]
    return pl.pallas_call(
        _in_proj_kernel,
        grid=(B, S // tm),
        in_specs=in_specs,
        out_specs=out_specs,
        out_shape=out_shape,
        scratch_shapes=[pltpu.VMEM((POOL_HALO + tm, POOL_WIDTH), F32)],
        compiler_params=pltpu.CompilerParams(
            dimension_semantics=("arbitrary", "arbitrary"), vmem_limit_bytes=VMEM_LIMIT),
        name="in_proj",
    )(x, cos_t, sin_t, cosT, sinT, g_mix, w_in_p, g_ql, w_qT, gq_col,
      g_kvl, w_k, w_vT, gk_nope, gk_grp, w_pool, pool_scale, g_qx, kT_mem, v_mem)


def _mla_attn_kernel(qT_ref, k_ref, vT_ref, o_ref, s_ref, mb_ref, p_ref, m_ref, acc_ref):
    n_q = k_ref.shape[1] // BQ
    chains = tuple(range(N_CHAIN))

    def scores(j, start, q0, cs):
        k = k_ref[0, pl.ds(start, BK), :]
        for c in cs:
            qT = qT_ref[0, :, pl.ds(pl.multiple_of(q0 + c * QC, QC), QC)]
            sT = _dot(k, qT)
            s_ref[j, c] = sT
            mb_ref[j, c] = jnp.max(sT, axis=0, keepdims=True)

    def mask(j, c, off):
        rr = lax.broadcasted_iota(jnp.int32, (BK, QC), 0)
        qq = lax.broadcasted_iota(jnp.int32, (BK, QC), 1)
        sT = jnp.where(rr <= qq + off, s_ref[j, c], -jnp.inf)
        s_ref[j, c] = sT
        mb_ref[j, c] = jnp.max(sT, axis=0, keepdims=True)

    def update(j, start, cs):
        for c in cs:
            m_old = m_ref[c]
            m_new = jnp.maximum(m_old, mb_ref[j, c])
            alpha = jnp.exp2(m_old - m_new)
            for r in range(0, BK, ROW_CHUNK):
                x = s_ref[j, c, r:r + ROW_CHUNK, :] - m_new
                p_ref[c, r:r + ROW_CHUNK, :] = jnp.exp2(x.astype(BF16))
            pv = _dot(vT_ref[0, :, pl.ds(start, BK)], p_ref[c])
            acc_ref[c] = acc_ref[c] * alpha + pv
            m_ref[c] = m_new

    in_blk = lambda d: tuple(c for c in chains if (c * QC) // BK == d)
    from_blk = lambda d: tuple(c for c in chains if (c * QC) // BK >= d)

    scores(0, 0, 0, chains)

    def qblock(i, carry):
        q0 = pl.multiple_of(i * BQ, BQ)
        m_ref[...] = jnp.full(m_ref.shape, -jnp.inf, F32)
        acc_ref[...] = jnp.zeros(acc_ref.shape, F32)

        def trip(t, carry):
            b0 = pl.multiple_of(t * (2 * BK), 2 * BK)
            scores(1, b0 + BK, q0, chains)
            update(0, b0, chains)

            @pl.when(t >= 0)
            def _():
                scores(0, b0 + 2 * BK, q0, chains)
                update(1, b0 + BK, chains)

            return carry

        lax.fori_loop(0, i, trip, 0)

        scores(1, q0 + BK, q0, from_blk(1))
        for c in in_blk(0):
            mask(0, c, c * QC)
        update(0, q0, chains)

        @pl.when(i >= 0)
        def _():
            q_next = pl.multiple_of(jnp.minimum(i + 1, n_q - 1) * BQ, BQ)
            scores(0, 0, q_next, chains)
            for c in in_blk(1):
                mask(1, c, c * QC - BK)
            update(1, q0 + BK, from_blk(1))
            for c in chains:
                o = acc_ref[c, 0:MLA_V_DIM, :] / acc_ref[c, MLA_V_DIM:MLA_V_DIM + 1, :]
                o_ref[0, pl.ds(pl.multiple_of(q0 + c * QC, QC), QC), :] = o.T.astype(BF16)

        return carry

    lax.fori_loop(0, n_q, qblock, 0)


def _mla_attn(qT, k, vT):
    B, S, _ = k.shape
    assert N_BLK == 2 and BQ % QC == 0 and BK % QC == 0 and BK % ROW_CHUNK == 0 and S % BQ == 0
    return pl.pallas_call(
        _mla_attn_kernel,
        grid=(B, MLA_HEADS),
        in_specs=[
            pl.BlockSpec((1, HEAD_PAD, S), lambda b, h: (b, h, 0)),
            pl.BlockSpec((1, S, HEAD_PAD), lambda b, h: (b, 0, h)),
            pl.BlockSpec((1, VT_ROWS, S), lambda b, h: (b, h, 0)),
        ],
        out_specs=pl.BlockSpec((1, S, MLA_V_DIM), lambda b, h: (b, 0, h)),
        out_shape=jax.ShapeDtypeStruct((B, S, MLA_HEADS * MLA_V_DIM), BF16),
        scratch_shapes=[
            pltpu.VMEM((2, N_CHAIN, BK, QC), F32),
            pltpu.VMEM((2, N_CHAIN, 1, QC), F32),
            pltpu.VMEM((N_CHAIN, BK, QC), BF16),
            pltpu.VMEM((N_CHAIN, 1, QC), F32),
            pltpu.VMEM((N_CHAIN, VT_ROWS, QC), F32),
        ],
        compiler_params=pltpu.CompilerParams(
            dimension_semantics=("arbitrary", "arbitrary"),
            vmem_limit_bytes=VMEM_LIMIT),
        name="mla_attn",
    )(qT, k, vT)


def _out_ffn_kernel(x_ref, yp_ref, ya_ref, ym_ref, w_o_ref, g_ffn_ref,
                    w_gate_ref, w_up_ref, conv_w_ref, conv_b_ref, w_down_ref,
                    o_ref, h2_ref, gbuf_ref, carry_ref):
    i = pl.program_id(1)
    c = pl.program_id(2)
    tm = x_ref.shape[1]
    n_mla = MLA_HEADS * MLA_V_DIM

    @pl.when(c == 0)
    def _():
        x1 = (x_ref[0]
              + _dot(yp_ref[0], w_o_ref[0:POOL_WIDTH, :])
              + _dot(ya_ref[0], w_o_ref[POOL_WIDTH:POOL_WIDTH + n_mla, :])
              + _dot(ym_ref[0], w_o_ref[POOL_WIDTH + n_mla:, :]))
        o_ref[0] = x1
        h2_ref[...] = _rms_rows(x1, g_ffn_ref[...]).astype(BF16)

        @pl.when(i == 0)
        def _():
            carry_ref[...] = jnp.zeros(carry_ref.shape, F32)

    h2 = h2_ref[...]
    g = _dot(h2, w_gate_ref[...])
    u = _dot(h2, w_up_ref[...])

    gbuf_ref[0:SUBLANES, :] = carry_ref[c]
    gbuf_ref[SUBLANES:SUBLANES + tm, :] = g
    carry_ref[c] = g[tm - SUBLANES:tm, :]
    gc = conv_b_ref[...] + conv_w_ref[CONV_WIDTH - 1:CONV_WIDTH, :] * g
    for j in range(CONV_WIDTH - 1):
        lag = CONV_WIDTH - 1 - j
        gc = gc + conv_w_ref[j:j + 1, :] * gbuf_ref[SUBLANES - lag:SUBLANES - lag + tm, :]
    act = (gc / (1.0 + jnp.exp(-gc))) * u
    o_ref[0] += _dot(act.astype(BF16), w_down_ref[...])


def _out_ffn(x, yp, ya, ym, w_o, g_ffn, w_gate, w_up, conv_w, conv_b, w_down):
    B, S, _ = x.shape
    tm = TM_FF
    n_mla = MLA_HEADS * MLA_V_DIM
    n_ff = D_FF // TF
    assert n_ff * TF == D_FF
    n_tiles = S // tm

    def tile_in(b, i, c):
        return (b, jnp.minimum(i + jnp.minimum(c, 1), n_tiles - 1), 0)

    return pl.pallas_call(
        _out_ffn_kernel,
        grid=(B, n_tiles, n_ff),
        in_specs=[
            pl.BlockSpec((1, tm, D_MODEL), tile_in),
            pl.BlockSpec((1, tm, POOL_WIDTH), tile_in),
            pl.BlockSpec((1, tm, n_mla), tile_in),
            pl.BlockSpec((1, tm, X_WIDTH), tile_in),
            _const_spec((D_MODEL, D_MODEL)),
            _const_spec((1, D_MODEL)),
            pl.BlockSpec((D_MODEL, TF), lambda b, i, c: (0, c)),
            pl.BlockSpec((D_MODEL, TF), lambda b, i, c: (0, c)),
            pl.BlockSpec((CONV_WIDTH, TF), lambda b, i, c: (0, c)),
            pl.BlockSpec((1, TF), lambda b, i, c: (0, c)),
            pl.BlockSpec((TF, D_MODEL), lambda b, i, c: (c, 0)),
        ],
        out_specs=pl.BlockSpec((1, tm, D_MODEL), lambda b, i, c: (b, i, 0)),
        out_shape=jax.ShapeDtypeStruct((B, S, D_MODEL), F32),
        scratch_shapes=[
            pltpu.VMEM((tm, D_MODEL), BF16),
            pltpu.VMEM((SUBLANES + tm, TF), F32),
            pltpu.VMEM((n_ff, SUBLANES, TF), F32),
        ],
        compiler_params=pltpu.CompilerParams(
            dimension_semantics=("arbitrary", "arbitrary", "arbitrary"),
            vmem_limit_bytes=VMEM_LIMIT),
        name="out_ffn",
    )(x, yp, ya, ym, w_o, g_ffn, w_gate, w_up, conv_w, conv_b, w_down)


def _dup_rope_cols(w):
    x1, x2 = w[..., :ROPE_HALF], w[..., ROPE_HALF:]
    return jnp.concatenate([x1, x2, x2, x1], axis=-1)


def _rope_tables(S):
    inv_freq = 1.0 / (ROPE_THETA ** (jnp.arange(ROPE_HALF, dtype=F32) / ROPE_HALF))
    ang = jnp.arange(S).astype(F32)[:, None] * inv_freq[None, :]
    cos, sin = jnp.cos(ang), jnp.sin(ang)
    zeros = jnp.zeros((S, 2 * ROPE_HALF), F32)
    return (jnp.concatenate([cos, cos, zeros], axis=-1),
            jnp.concatenate([-sin, sin, zeros], axis=-1), cos.T, sin.T)


def _layer(x, mem, g_mix, w_in, g_q_lat, w_q_up, g_kv_lat, w_kv_up, g_q_mla, g_k_mla,
           w_pool, pool_scale, g_mem, w_mem_kv, g_q_x, g_k_x, w_o, g_ffn,
           w_gate, w_up, conv_w, conv_b, w_down):
    S = x.shape[1]
    row = lambda v: v.reshape(1, -1)

    s0, s1, s2, s3 = _C_QLAT, _C_KVLAT, _C_KR, _C_KR + MLA_ROPE_DIM
    w_in_p = jnp.concatenate(
        [w_in[:, :s2], _dup_rope_cols(w_in[:, s2:s3]), w_in[:, s3:]], axis=1).astype(BF16)

    wq = w_q_up.reshape(Q_LORA_RANK, MLA_HEADS, MLA_QK_DIM)
    w_qT = jnp.pad(wq, ((0, 0), (0, 0), (0, HEAD_PAD - MLA_QK_DIM))).reshape(
        Q_LORA_RANK, MLA_HEADS * HEAD_PAD).T.astype(BF16)

    wkv = w_kv_up.reshape(KV_LORA_RANK, MLA_HEADS, MLA_NOPE_DIM + MLA_V_DIM)
    w_k = wkv[..., :MLA_NOPE_DIM].reshape(KV_LORA_RANK, -1).astype(BF16)
    w_vT = wkv[..., MLA_NOPE_DIM:].reshape(KV_LORA_RANK, -1).T.astype(BF16)

    cos_t, sin_t, cosT, sinT = _rope_tables(S)

    kT_mem, v_mem = _mem_kv(
        mem, row(g_mem), w_mem_kv[:, :X_WIDTH].T.astype(BF16), w_mem_kv[:, X_WIDTH:].astype(BF16),
        g_k_x.reshape(-1, 1))

    qT, k, vT, yp, ym = _in_proj(
        x, cos_t, sin_t, cosT, sinT, row(g_mix), w_in_p,
        row(g_q_lat), w_qT, g_q_mla.reshape(-1, 1),
        row(g_kv_lat), w_k, w_vT, row(g_k_mla[:MLA_NOPE_DIM]), row(_dup_rope_cols(g_k_mla[MLA_NOPE_DIM:])),
        w_pool.astype(BF16), row(pool_scale), row(g_q_x), kT_mem, v_mem)

    ya = _mla_attn(qT, k, vT)

    return _out_ffn(x, yp, ya, ym, w_o.astype(BF16), row(g_ffn),
                    w_gate.astype(BF16), w_up.astype(BF16), conv_w, row(conv_b), w_down.astype(BF16))


def kernel(x, mem, g_mix, w_in, g_q_lat, w_q_up, g_kv_lat, w_kv_up, g_q_mla, g_k_mla, w_pool, pool_scale, g_mem, w_mem_kv, g_q_x, g_k_x, w_o, g_ffn, w_gate, w_up, conv_w, conv_b, w_down):
    depth = g_mix.shape[0]
    for l in range(depth):
        x = _layer(x, mem, g_mix[l], w_in[l], g_q_lat[l], w_q_up[l], g_kv_lat[l], w_kv_up[l],
                   g_q_mla[l], g_k_mla[l], w_pool[l], pool_scale[l], g_mem[l], w_mem_kv[l],
                   g_q_x[l], g_k_x[l], w_o[l], g_ffn[l], w_gate[l], w_up[l], conv_w[l],
                   conv_b[l], w_down[l])
    return x
```

```python
import functools
import math

import jax
import jax.numpy as jnp
from jax import lax
from jax.experimental import pallas as pl
from jax.experimental.pallas import tpu as pltpu

D_MODEL = 2048
MEM_LEN = 256
POOL_WIDTH = 512
POOL_WINDOWS = (2, 4, 8, 16)
POOL_GROUP_DIM = 128
POOL_HALO = 16
MLA_HEADS = 8
MLA_V_DIM = 128
MLA_NOPE_DIM = 128
MLA_ROPE_DIM = 64
MLA_QK_DIM = MLA_NOPE_DIM + MLA_ROPE_DIM
Q_LORA_RANK = 512
KV_LORA_RANK = 256
X_HEADS = 4
X_WIDTH = 512
X_HEAD_DIM = 128
D_FF = 5632
CONV_WIDTH = 3
ROPE_THETA = 10000.0
NORM_EPS = 1e-6

LANES = 128
SUBLANES = 8
HEAD_PAD = 2 * LANES
VT_ROWS = MLA_V_DIM + 16
ROPE_HALF = MLA_ROPE_DIM // 2

_C_POOL = 0
_C_QLAT = _C_POOL + POOL_WIDTH
_C_KVLAT = _C_QLAT + Q_LORA_RANK
_C_KR = _C_KVLAT + KV_LORA_RANK
_C_MQ = _C_KR + LANES
IN_COLS_PAD = _C_MQ + X_WIDTH

TM_IN = 512
BQ = 2048
QC = 256
N_CHAIN = BQ // QC
N_BLK = 2
BK = BQ // N_BLK
ROW_CHUNK = 32
TM_FF = 512
TF = 512
VMEM_LIMIT = 56 * 1024 * 1024

BF16 = jnp.bfloat16
F32 = jnp.float32


def _dot(a, b):
    return jnp.dot(a, b, preferred_element_type=F32)


def _dot_nt(a, b):
    return lax.dot_general(a, b, (((1,), (1,)), ((), ())), preferred_element_type=F32)


def _rms_rows(xf, g):
    return xf * lax.rsqrt(jnp.mean(xf * xf, axis=-1, keepdims=True) + NORM_EPS) * g


def _rope_group(grp, cos_t, sin_t):
    return grp * cos_t + pltpu.roll(grp, 2 * ROPE_HALF, 1) * sin_t


def _const_spec(shape):
    nd = len(shape)
    return pl.BlockSpec(shape, lambda *_: (0,) * nd, pipeline_mode=pl.Buffered(1))


def _mem_kv_kernel(mem_ref, g_mem_ref, wkT_ref, wv_ref, gk_ref, kT_ref, v_ref):
    h = _rms_rows(mem_ref[0], g_mem_ref[...]).astype(BF16)
    kT = _dot_nt(wkT_ref[...], h)
    for hd in range(X_HEADS):
        rows = slice(hd * X_HEAD_DIM, (hd + 1) * X_HEAD_DIM)
        kh = kT[rows, :]
        r = lax.rsqrt(jnp.mean(kh * kh, axis=0, keepdims=True) + NORM_EPS)
        kT_ref[0, rows, :] = (kh * r * gk_ref[...]).astype(BF16)
    v_ref[0] = _dot(h, wv_ref[...]).astype(BF16)


def _mem_kv(mem, g_mem, wkT, wv, gk_col):
    B = mem.shape[0]
    return pl.pallas_call(
        _mem_kv_kernel,
        grid=(B,),
        in_specs=[
            pl.BlockSpec((1, MEM_LEN, D_MODEL), lambda b: (b, 0, 0)),
            _const_spec((1, D_MODEL)),
            _const_spec((X_WIDTH, D_MODEL)),
            _const_spec((D_MODEL, X_WIDTH)),
            _const_spec((X_HEAD_DIM, 1)),
        ],
        out_specs=[
            pl.BlockSpec((1, X_WIDTH, MEM_LEN), lambda b: (b, 0, 0)),
            pl.BlockSpec((1, MEM_LEN, X_WIDTH), lambda b: (b, 0, 0)),
        ],
        out_shape=[
            jax.ShapeDtypeStruct((B, X_WIDTH, MEM_LEN), BF16),
            jax.ShapeDtypeStruct((B, MEM_LEN, X_WIDTH), BF16),
        ],
        compiler_params=pltpu.CompilerParams(
            dimension_semantics=("arbitrary",), vmem_limit_bytes=VMEM_LIMIT),
        name="mem_kv",
    )(mem, g_mem, wkT, wv, gk_col)


def _in_proj_kernel(x_ref, cos_ref, sin_ref, cosT_ref, sinT_ref, g_mix_ref, w_in_ref,
                    g_ql_ref, w_qT_ref, gq_col_ref,
                    g_kvl_ref, w_k_ref, w_vT_ref, gk_nope_ref, gk_grp_ref,
                    w_pool_ref, pool_scale_ref,
                    g_qx_ref, kT_mem_ref, v_mem_ref,
                    qT_ref, k_ref, vT_ref, yp_ref, ym_ref,
                    zp_ref):
    i = pl.program_id(1)
    tm = x_ref.shape[1]
    cos_t = cos_ref[...]
    sin_t = sin_ref[...]

    @pl.when(i == 0)
    def _():
        zp_ref[0:POOL_HALO, :] = jnp.zeros((POOL_HALO, POOL_WIDTH), F32)

    @pl.when(i > 0)
    def _():
        zp_ref[0:POOL_HALO, :] = zp_ref[tm:tm + POOL_HALO, :]

    h = _rms_rows(x_ref[0], g_mix_ref[...]).astype(BF16)
    z = _dot(h, w_in_ref[...])
    z_pool = z[:, _C_POOL:_C_POOL + POOL_WIDTH]
    z_q = z[:, _C_QLAT:_C_QLAT + Q_LORA_RANK]
    z_kv = z[:, _C_KVLAT:_C_KVLAT + KV_LORA_RANK]
    kr = z[:, _C_KR:_C_KR + LANES]
    z_mq = z[:, _C_MQ:_C_MQ + X_WIDTH]

    zp_ref[POOL_HALO:POOL_HALO + tm, :] = z_pool
    t = i * tm + lax.broadcasted_iota(jnp.int32, (tm, 1), 0)
    for g, w in enumerate(POOL_WINDOWS):
        cols = slice(g * POOL_GROUP_DIM, (g + 1) * POOL_GROUP_DIM)
        s = z_pool[:, cols]
        for j in range(1, w):
            s = s + zp_ref[POOL_HALO - j:POOL_HALO - j + tm, cols]
        cnt = jnp.minimum(t + 1, w).astype(F32)
        d = s / cnt - z_pool[:, cols]
        y = _dot(d.astype(BF16), w_pool_ref[g]) * pool_scale_ref[:, cols]
        yp_ref[0, :, cols] = y.astype(BF16)

    q_scale = math.log2(math.e) / math.sqrt(MLA_QK_DIM)
    hq = _rms_rows(z_q, g_ql_ref[...]).astype(BF16)
    qfT = _dot_nt(w_qT_ref[...], hq)
    cosT = cosT_ref[...]
    sinT = sinT_ref[...]
    for hd in range(MLA_HEADS):
        r0 = hd * HEAD_PAD
        raw = qfT[r0:r0 + MLA_QK_DIM, :]
        ssq = jnp.sum(raw * raw, axis=0, keepdims=True)
        r = lax.rsqrt(ssq * (1.0 / MLA_QK_DIM) + NORM_EPS) * q_scale
        g = raw * gq_col_ref[...]
        x1 = g[MLA_NOPE_DIM:MLA_NOPE_DIM + ROPE_HALF, :]
        x2 = g[MLA_NOPE_DIM + ROPE_HALF:MLA_QK_DIM, :]
        qT_ref[0, r0:r0 + MLA_NOPE_DIM, :] = (g[0:MLA_NOPE_DIM, :] * r).astype(BF16)
        qT_ref[0, r0 + MLA_NOPE_DIM:r0 + MLA_NOPE_DIM + ROPE_HALF, :] = (
            (x1 * cosT - x2 * sinT) * r).astype(BF16)
        qT_ref[0, r0 + MLA_NOPE_DIM + ROPE_HALF:r0 + MLA_QK_DIM, :] = (
            (x2 * cosT + x1 * sinT) * r).astype(BF16)
        qT_ref[0, r0 + MLA_QK_DIM:r0 + HEAD_PAD, :] = jnp.zeros((HEAD_PAD - MLA_QK_DIM, tm), BF16)

    hkv = _rms_rows(z_kv, g_kvl_ref[...]).astype(BF16)
    kn = _dot(hkv, w_k_ref[...])
    vT = _dot_nt(w_vT_ref[...], hkv).astype(BF16)
    for hd in range(MLA_HEADS):
        r0 = hd * VT_ROWS
        vT_ref[0, r0:r0 + MLA_V_DIM, :] = vT[hd * MLA_V_DIM:(hd + 1) * MLA_V_DIM, :]
        vT_ref[0, r0 + MLA_V_DIM:r0 + VT_ROWS, :] = jnp.ones((VT_ROWS - MLA_V_DIM, tm), BF16)
    kr_ssq = 0.5 * jnp.sum(kr * kr, axis=-1, keepdims=True)
    kr_rot = _rope_group(kr * gk_grp_ref[...], cos_t, sin_t)
    for hd in range(MLA_HEADS):
        nope = kn[:, hd * LANES:(hd + 1) * LANES]
        ssq = jnp.sum(nope * nope, axis=-1, keepdims=True) + kr_ssq
        r = lax.rsqrt(ssq * (1.0 / MLA_QK_DIM) + NORM_EPS)
        c0 = hd * HEAD_PAD
        k_ref[0, :, c0:c0 + LANES] = (nope * r * gk_nope_ref[...]).astype(BF16)
        k_ref[0, :, c0 + LANES:c0 + HEAD_PAD] = (kr_rot * r).astype(BF16)

    x_scale = 1.0 / math.sqrt(X_HEAD_DIM)
    for hd in range(X_HEADS):
        cols = slice(hd * X_HEAD_DIM, (hd + 1) * X_HEAD_DIM)
        qn = (_rms_rows(z_mq[:, cols], g_qx_ref[...]) * x_scale).astype(BF16)
        s = _dot(qn, kT_mem_ref[0, cols, :])
        m = jnp.max(s, axis=-1, keepdims=True)
        p = jnp.exp(s - m)
        l = jnp.sum(p, axis=-1, keepdims=True)
        o = _dot(p.astype(BF16), v_mem_ref[0, :, cols]) / l
        ym_ref[0, :, cols] = o.astype(BF16)


def _in_proj(x, cos_t, sin_t, cosT, sinT, g_mix, w_in_p, g_ql, w_qT, gq_col,
             g_kvl, w_k, w_vT, gk_nope, gk_grp, w_pool, pool_scale, g_qx, kT_mem, v_mem):
    B, S, _ = x.shape
    tm = TM_IN
    n_kv = MLA_HEADS * MLA_V_DIM
    in_specs = [
        pl.BlockSpec((1, tm, D_MODEL), lambda b, i: (b, i, 0)),
        pl.BlockSpec((tm, LANES), lambda b, i: (i, 0)),
        pl.BlockSpec((tm, LANES), lambda b, i: (i, 0)),
        pl.BlockSpec((ROPE_HALF, tm), lambda b, i: (0, i)),
        pl.BlockSpec((ROPE_HALF, tm), lambda b, i: (0, i)),
        _const_spec((1, D_MODEL)),
        _const_spec((D_MODEL, IN_COLS_PAD)),
        _const_spec((1, Q_LORA_RANK)),
        _const_spec((MLA_HEADS * HEAD_PAD, Q_LORA_RANK)),
        _const_spec((MLA_QK_DIM, 1)),
        _const_spec((1, KV_LORA_RANK)),
        _const_spec((KV_LORA_RANK, n_kv)),
        _const_spec((n_kv, KV_LORA_RANK)),
        _const_spec((1, LANES)),
        _const_spec((1, LANES)),
        _const_spec((len(POOL_WINDOWS), POOL_GROUP_DIM, POOL_GROUP_DIM)),
        _const_spec((1, POOL_WIDTH)),
        _const_spec((1, X_HEAD_DIM)),
        pl.BlockSpec((1, X_WIDTH, MEM_LEN), lambda b, i: (b, 0, 0)),
        pl.BlockSpec((1, MEM_LEN, X_WIDTH), lambda b, i: (b, 0, 0)),
    ]
    out_specs = [
        pl.BlockSpec((1, MLA_HEADS * HEAD_PAD, tm), lambda b, i: (b, 0, i)),
        pl.BlockSpec((1, tm, MLA_HEADS * HEAD_PAD), lambda b, i: (b, i, 0)),
        pl.BlockSpec((1, MLA_HEADS * VT_ROWS, tm), lambda b, i: (b, 0, i)),
        pl.BlockSpec((1, tm, POOL_WIDTH), lambda b, i: (b, i, 0)),
        pl.BlockSpec((1, tm, X_WIDTH), lambda b, i: (b, i, 0)),
    ]
    out_shape = [
        jax.ShapeDtypeStruct((B, MLA_HEADS * HEAD_PAD, S), BF16),
        jax.ShapeDtypeStruct((B, S, MLA_HEADS * HEAD_PAD), BF16),
        jax.ShapeDtypeStruct((B, MLA_HEADS * VT_ROWS, S), BF16),
        jax.ShapeDtypeStruct((B, S, POOL_WIDTH), BF16),
        jax.ShapeDtypeStruct((B, S, X_WIDTH), BF16),
    ]
    return pl.pallas_call(
        _in_proj_kernel,
        grid=(B, S // tm),
        in_specs=in_specs,
        out_specs=out_specs,
        out_shape=out_shape,
        scratch_shapes=[pltpu.VMEM((POOL_HALO + tm, POOL_WIDTH), F32)],
        compiler_params=pltpu.CompilerParams(
            dimension_semantics=("arbitrary", "arbitrary"), vmem_limit_bytes=VMEM_LIMIT),
        name="in_proj",
    )(x, cos_t, sin_t, cosT, sinT, g_mix, w_in_p, g_ql, w_qT, gq_col,
      g_kvl, w_k, w_vT, gk_nope, gk_grp, w_pool, pool_scale, g_qx, kT_mem, v_mem)


def _mla_attn_kernel(qT_ref, k_ref, vT_ref, o_ref, s_ref, mb_ref, p_ref, m_ref, acc_ref):
    n_q = k_ref.shape[1] // BQ
    chains = tuple(range(N_CHAIN))

    in_blk = lambda d: tuple(c for c in chains if (c * QC) // BK == d)
    from_blk = lambda d: tuple(c for c in chains if (c * QC) // BK >= d)
    vis = lambda c, d: c * QC - d * BK + QC if (c * QC) // BK == d else BK
    full = lambda c: BK

    def scores(j, start, q0, cs, rows=full):
        for c in cs:
            k = k_ref[0, pl.ds(start, rows(c)), :]
            qT = qT_ref[0, :, pl.ds(pl.multiple_of(q0 + c * QC, QC), QC)]
            sT = _dot(k, qT)
            s_ref[j, c, 0:rows(c), :] = sT
            mb_ref[j, c] = jnp.max(sT, axis=0, keepdims=True)

    def mask_diag(j, c, n):
        rr = lax.broadcasted_iota(jnp.int32, (QC, QC), 0)
        qq = lax.broadcasted_iota(jnp.int32, (QC, QC), 1)
        sT = jnp.where(rr <= qq, s_ref[j, c, n - QC:n, :], -jnp.inf)
        s_ref[j, c, n - QC:n, :] = sT
        mb = jnp.max(sT, axis=0, keepdims=True)
        if n > QC:
            mb = jnp.maximum(mb, jnp.max(s_ref[j, c, 0:n - QC, :], axis=0, keepdims=True))
        mb_ref[j, c] = mb

    def update(j, start, cs, rows=full):
        for c in cs:
            n = rows(c)
            m_old = m_ref[c]
            m_new = jnp.maximum(m_old, mb_ref[j, c])
            alpha = jnp.exp2(m_old - m_new)
            for r in range(0, n, ROW_CHUNK):
                x = s_ref[j, c, r:r + ROW_CHUNK, :] - m_new
                p_ref[c, r:r + ROW_CHUNK, :] = jnp.exp2(x.astype(BF16))
            pv = _dot(vT_ref[0, :, pl.ds(start, n)], p_ref[c, 0:n, :])
            acc_ref[c] = acc_ref[c] * alpha + pv
            m_ref[c] = m_new

    scores(0, 0, 0, chains)

    def qblock(i, carry):
        q0 = pl.multiple_of(i * BQ, BQ)
        m_ref[...] = jnp.full(m_ref.shape, -jnp.inf, F32)
        acc_ref[...] = jnp.zeros(acc_ref.shape, F32)

        def trip(t, carry):
            b0 = pl.multiple_of(t * (2 * BK), 2 * BK)
            scores(1, b0 + BK, q0, chains)
            update(0, b0, chains)

            @pl.when(t >= 0)
            def _():
                scores(0, b0 + 2 * BK, q0, chains)
                update(1, b0 + BK, chains)

            return carry

        lax.fori_loop(0, i, trip, 0)

        scores(1, q0 + BK, q0, from_blk(1), lambda c: vis(c, 1))
        for c in in_blk(0):
            mask_diag(0, c, vis(c, 0))
        update(0, q0, chains, lambda c: vis(c, 0))

        @pl.when(i >= 0)
        def _():
            q_next = pl.multiple_of(jnp.minimum(i + 1, n_q - 1) * BQ, BQ)
            scores(0, 0, q_next, chains)
            for c in in_blk(1):
                mask_diag(1, c, vis(c, 1))
            update(1, q0 + BK, from_blk(1), lambda c: vis(c, 1))
            for c in chains:
                o = acc_ref[c, 0:MLA_V_DIM, :] / acc_ref[c, MLA_V_DIM:MLA_V_DIM + 1, :]
                o_ref[0, pl.ds(pl.multiple_of(q0 + c * QC, QC), QC), :] = o.T.astype(BF16)

        return carry

    lax.fori_loop(0, n_q, qblock, 0)


def _mla_attn(qT, k, vT):
    B, S, _ = k.shape
    assert N_BLK == 2 and BQ % QC == 0 and BK % QC == 0 and BK % ROW_CHUNK == 0 and S % BQ == 0
    return pl.pallas_call(
        _mla_attn_kernel,
        grid=(B, MLA_HEADS),
        in_specs=[
            pl.BlockSpec((1, HEAD_PAD, S), lambda b, h: (b, h, 0)),
            pl.BlockSpec((1, S, HEAD_PAD), lambda b, h: (b, 0, h)),
            pl.BlockSpec((1, VT_ROWS, S), lambda b, h: (b, h, 0)),
        ],
        out_specs=pl.BlockSpec((1, S, MLA_V_DIM), lambda b, h: (b, 0, h)),
        out_shape=jax.ShapeDtypeStruct((B, S, MLA_HEADS * MLA_V_DIM), BF16),
        scratch_shapes=[
            pltpu.VMEM((2, N_CHAIN, BK, QC), F32),
            pltpu.VMEM((2, N_CHAIN, 1, QC), F32),
            pltpu.VMEM((N_CHAIN, BK, QC), BF16),
            pltpu.VMEM((N_CHAIN, 1, QC), F32),
            pltpu.VMEM((N_CHAIN, VT_ROWS, QC), F32),
        ],
        compiler_params=pltpu.CompilerParams(
            dimension_semantics=("arbitrary", "arbitrary"),
            vmem_limit_bytes=VMEM_LIMIT),
        name="mla_attn",
    )(qT, k, vT)


def _out_ffn_kernel(x_ref, yp_ref, ya_ref, ym_ref, w_o_ref, g_ffn_ref,
                    w_gate_ref, w_up_ref, conv_w_ref, conv_b_ref, w_down_ref,
                    o_ref, h2_ref, gbuf_ref, carry_ref):
    i = pl.program_id(1)
    c = pl.program_id(2)
    tm = x_ref.shape[1]
    n_mla = MLA_HEADS * MLA_V_DIM

    @pl.when(c == 0)
    def _():
        x1 = (x_ref[0]
              + _dot(yp_ref[0], w_o_ref[0:POOL_WIDTH, :])
              + _dot(ya_ref[0], w_o_ref[POOL_WIDTH:POOL_WIDTH + n_mla, :])
              + _dot(ym_ref[0], w_o_ref[POOL_WIDTH + n_mla:, :]))
        o_ref[0] = x1
        h2_ref[...] = _rms_rows(x1, g_ffn_ref[...]).astype(BF16)

        @pl.when(i == 0)
        def _():
            carry_ref[...] = jnp.zeros(carry_ref.shape, F32)

    h2 = h2_ref[...]
    g = _dot(h2, w_gate_ref[...])
    u = _dot(h2, w_up_ref[...])

    gbuf_ref[0:SUBLANES, :] = carry_ref[c]
    gbuf_ref[SUBLANES:SUBLANES + tm, :] = g
    carry_ref[c] = g[tm - SUBLANES:tm, :]
    gc = conv_b_ref[...] + conv_w_ref[CONV_WIDTH - 1:CONV_WIDTH, :] * g
    for j in range(CONV_WIDTH - 1):
        lag = CONV_WIDTH - 1 - j
        gc = gc + conv_w_ref[j:j + 1, :] * gbuf_ref[SUBLANES - lag:SUBLANES - lag + tm, :]
    act = (gc / (1.0 + jnp.exp(-gc))) * u
    o_ref[0] += _dot(act.astype(BF16), w_down_ref[...])


def _out_ffn(x, yp, ya, ym, w_o, g_ffn, w_gate, w_up, conv_w, conv_b, w_down):
    B, S, _ = x.shape
    tm = TM_FF
    n_mla = MLA_HEADS * MLA_V_DIM
    n_ff = D_FF // TF
    assert n_ff * TF == D_FF
    n_tiles = S // tm

    def tile_in(b, i, c):
        return (b, jnp.minimum(i + jnp.minimum(c, 1), n_tiles - 1), 0)

    return pl.pallas_call(
        _out_ffn_kernel,
        grid=(B, n_tiles, n_ff),
        in_specs=[
            pl.BlockSpec((1, tm, D_MODEL), tile_in),
            pl.BlockSpec((1, tm, POOL_WIDTH), tile_in),
            pl.BlockSpec((1, tm, n_mla), tile_in),
            pl.BlockSpec((1, tm, X_WIDTH), tile_in),
            _const_spec((D_MODEL, D_MODEL)),
            _const_spec((1, D_MODEL)),
            pl.BlockSpec((D_MODEL, TF), lambda b, i, c: (0, c)),
            pl.BlockSpec((D_MODEL, TF), lambda b, i, c: (0, c)),
            pl.BlockSpec((CONV_WIDTH, TF), lambda b, i, c: (0, c)),
            pl.BlockSpec((1, TF), lambda b, i, c: (0, c)),
            pl.BlockSpec((TF, D_MODEL), lambda b, i, c: (c, 0)),
        ],
        out_specs=pl.BlockSpec((1, tm, D_MODEL), lambda b, i, c: (b, i, 0)),
        out_shape=jax.ShapeDtypeStruct((B, S, D_MODEL), F32),
        scratch_shapes=[
            pltpu.VMEM((tm, D_MODEL), BF16),
            pltpu.VMEM((SUBLANES + tm, TF), F32),
            pltpu.VMEM((n_ff, SUBLANES, TF), F32),
        ],
        compiler_params=pltpu.CompilerParams(
            dimension_semantics=("arbitrary", "arbitrary", "arbitrary"),
            vmem_limit_bytes=VMEM_LIMIT),
        name="out_ffn",
    )(x, yp, ya, ym, w_o, g_ffn, w_gate, w_up, conv_w, conv_b, w_down)


def _dup_rope_cols(w):
    x1, x2 = w[..., :ROPE_HALF], w[..., ROPE_HALF:]
    return jnp.concatenate([x1, x2, x2, x1], axis=-1)


def _rope_tables(S):
    inv_freq = 1.0 / (ROPE_THETA ** (jnp.arange(ROPE_HALF, dtype=F32) / ROPE_HALF))
    ang = jnp.arange(S).astype(F32)[:, None] * inv_freq[None, :]
    cos, sin = jnp.cos(ang), jnp.sin(ang)
    zeros = jnp.zeros((S, 2 * ROPE_HALF), F32)
    return (jnp.concatenate([cos, cos, zeros], axis=-1),
            jnp.concatenate([-sin, sin, zeros], axis=-1), cos.T, sin.T)


def _layer(x, mem, g_mix, w_in, g_q_lat, w_q_up, g_kv_lat, w_kv_up, g_q_mla, g_k_mla,
           w_pool, pool_scale, g_mem, w_mem_kv, g_q_x, g_k_x, w_o, g_ffn,
           w_gate, w_up, conv_w, conv_b, w_down):
    S = x.shape[1]
    row = lambda v: v.reshape(1, -1)

    s2, s3 = _C_KR, _C_KR + MLA_ROPE_DIM
    w_in_b = w_in.astype(BF16)
    w_in_p = jnp.concatenate(
        [w_in_b[:, :s2], _dup_rope_cols(w_in_b[:, s2:s3]), w_in_b[:, s3:]], axis=1)

    wq = w_q_up.astype(BF16).reshape(Q_LORA_RANK, MLA_HEADS, MLA_QK_DIM)
    w_qT = jnp.pad(wq, ((0, 0), (0, 0), (0, HEAD_PAD - MLA_QK_DIM))).reshape(
        Q_LORA_RANK, MLA_HEADS * HEAD_PAD).T

    wkv = w_kv_up.astype(BF16).reshape(KV_LORA_RANK, MLA_HEADS, MLA_NOPE_DIM + MLA_V_DIM)
    w_k = wkv[..., :MLA_NOPE_DIM].reshape(KV_LORA_RANK, -1)
    w_vT = wkv[..., MLA_NOPE_DIM:].reshape(KV_LORA_RANK, -1).T

    cos_t, sin_t, cosT, sinT = _rope_tables(S)

    w_mem_b = w_mem_kv.astype(BF16)
    kT_mem, v_mem = _mem_kv(
        mem, row(g_mem), w_mem_b[:, :X_WIDTH].T, w_mem_b[:, X_WIDTH:], g_k_x.reshape(-1, 1))

    qT, k, vT, yp, ym = _in_proj(
        x, cos_t, sin_t, cosT, sinT, row(g_mix), w_in_p,
        row(g_q_lat), w_qT, g_q_mla.reshape(-1, 1),
        row(g_kv_lat), w_k, w_vT, row(g_k_mla[:MLA_NOPE_DIM]), row(_dup_rope_cols(g_k_mla[MLA_NOPE_DIM:])),
        w_pool.astype(BF16), row(pool_scale), row(g_q_x), kT_mem, v_mem)

    ya = _mla_attn(qT, k, vT)

    return _out_ffn(x, yp, ya, ym, w_o.astype(BF16), row(g_ffn),
                    w_gate.astype(BF16), w_up.astype(BF16), conv_w, row(conv_b), w_down.astype(BF16))


def kernel(x, mem, g_mix, w_in, g_q_lat, w_q_up, g_kv_lat, w_kv_up, g_q_mla, g_k_mla, w_pool, pool_scale, g_mem, w_mem_kv, g_q_x, g_k_x, w_o, g_ffn, w_gate, w_up, conv_w, conv_b, w_down):
    depth = g_mix.shape[0]
    for l in range(depth):
        x = _layer(x, mem, g_mix[l], w_in[l], g_q_lat[l], w_q_up[l], g_kv_lat[l], w_kv_up[l],
                   g_q_mla[l], g_k_mla[l], w_pool[l], pool_scale[l], g_mem[l], w_mem_kv[l],
                   g_q_x[l], g_k_x[l], w_o[l], g_ffn[l], w_gate[l], w_up[l], conv_w[l],
                   conv_b[l], w_down[l])
    return x
```
